```python
import jax
import jax.numpy as jnp
from jax import lax
import numpy as np

D_MODEL = 2048
BATCH = 4
SEQ = 4096
DEPTH = 2

GRID_W = 64
CTX_LEN = 256
D_GROUP = D_MODEL // 2
CONF_KERNEL = 31
LRU_CONV = 4
LRU_BLOCKS = 16
LRU_BLOCK_DIM = D_GROUP // LRU_BLOCKS
LRU_C = 8.0
NA_HEADS = 8
NA_KH = 8
NA_KW = 16
HG_HEADS = 8
HG_HEAD_DIM = D_GROUP // HG_HEADS
HG_CHUNK = 64
N_EXPERTS = 16
D_FF_EXPERT = 2048
EC_FACTOR = 2
N_AB = (DEPTH + 1) // 2
N_CD = DEPTH // 2
EPS = 1e-6
NEG_INF = -1e30

kernel_name = "hybrid_conv_rglru_natten_hgrn2_ecmoe"


def rmsnorm(x, g):
    xf = x.astype(jnp.float32)
    y = xf * lax.rsqrt(jnp.mean(xf * xf, axis=-1, keepdims=True) + EPS)
    return (y * g.astype(jnp.float32)).astype(x.dtype)


def layernorm(x, g, b):
    xf = x.astype(jnp.float32)
    mu = jnp.mean(xf, axis=-1, keepdims=True)
    var = jnp.mean(jnp.square(xf - mu), axis=-1, keepdims=True)
    y = (xf - mu) * lax.rsqrt(var + EPS)
    return (y * g.astype(jnp.float32) + b.astype(jnp.float32)).astype(x.dtype)


def modulate(h, shift, scale):
    return h * (1.0 + scale) + shift


def split_heads(t, n_heads):
    b, n, _ = t.shape
    return t.reshape(b, n, n_heads, -1).transpose(0, 2, 1, 3)


def merge_heads(t):
    b, h, n, d = t.shape
    return t.transpose(0, 2, 1, 3).reshape(b, n, h * d)


def depthwise_conv(x, w, b, pad_left, pad_right):
    y = lax.conv_general_dilated(
        x, w[:, None, :].astype(x.dtype), (1,), [(pad_left, pad_right)],
        dimension_numbers=("NWC", "WIO", "NWC"), feature_group_count=x.shape[-1])
    return y + b


def rglru_coeffs(x, w_a, b_a, w_i, b_i, lam):
    bn, n, _ = x.shape
    xb = x.reshape(bn, n, LRU_BLOCKS, LRU_BLOCK_DIM)
    r = jax.nn.sigmoid(jnp.einsum("bnhi,hij->bnhj", xb, w_a).reshape(bn, n, D_GROUP) + b_a)
    i = jax.nn.sigmoid(jnp.einsum("bnhi,hij->bnhj", xb, w_i).reshape(bn, n, D_GROUP) + b_i)
    log_a = (-LRU_C * r.astype(jnp.float32)) * jax.nn.softplus(-lam.astype(jnp.float32))
    a = jnp.exp(log_a)
    b = jnp.sqrt(-jnp.expm1(2.0 * log_a)) * (i * x).astype(jnp.float32)
    return a, b


def linear_scan(a, b, h0):
    def combine(e1, e2):
        a1, b1 = e1
        a2, b2 = e2
        return a1 * a2, a2 * b1 + b2
    a_cum, b_cum = lax.associative_scan(combine, (a, b), axis=1)
    return a_cum * h0[:, None, :] + b_cum


def rglru_direction(xl, xc, w_a, b_a, w_i, b_i, lam, reverse):
    if reverse:
        xl, xc = xl[:, ::-1], xc[:, ::-1]
    ac, bc = rglru_coeffs(xc, w_a, b_a, w_i, b_i, lam)
    hc = linear_scan(ac, bc, jnp.zeros_like(bc[:, 0]))
    al, bl = rglru_coeffs(xl, w_a, b_a, w_i, b_i, lam)
    hl = linear_scan(al, bl, hc[:, -1])
    if reverse:
        hl, hc = hl[:, ::-1], hc[:, ::-1]
    return hl.astype(xl.dtype), hc.astype(xc.dtype)


def ab_mixer(h, hc, w_in, w_out, dw_w, dw_b, ln_g, ln_b, cv_w, cv_b, wa, ba, wi, bi, lam, need_ctx):
    za, zg, zy, zx = jnp.split(h @ w_in, 4, axis=-1)
    ca, cg, cy, cx = jnp.split(hc @ w_in, 4, axis=-1)

    def conformer(a_val, a_gate):
        u = a_val * jax.nn.sigmoid(a_gate)
        u = depthwise_conv(u, dw_w, dw_b, CONF_KERNEL // 2, CONF_KERNEL // 2)
        return jax.nn.silu(layernorm(u, ln_g, ln_b))

    xl = depthwise_conv(zx, cv_w, cv_b, 2, 1)
    xcc = depthwise_conv(cx, cv_w, cv_b, 2, 1)
    f_l, f_c = rglru_direction(xl, xcc, wa[0], ba[0], wi[0], bi[0], lam[0], False)
    b_l, b_c = rglru_direction(xl, xcc, wa[1], ba[1], wi[1], bi[1], lam[1], True)
    out_l = jnp.concatenate([conformer(za, zg), jax.nn.gelu(zy) * (f_l + b_l)], axis=-1) @ w_out
    if not need_ctx:
        return out_l, None
    out_c = jnp.concatenate([conformer(ca, cg), jax.nn.gelu(cy) * (f_c + b_c)], axis=-1) @ w_out
    return out_l, out_c


def dense_attention(q, k, v):
    s = jnp.einsum("bhqd,bhkd->bhqk", q, k).astype(jnp.float32) * (q.shape[-1] ** -0.5)
    return jnp.einsum("bhqk,bhkd->bhqd", jax.nn.softmax(s, axis=-1).astype(v.dtype), v)


def na_latent(q, k, v, kc, vc, table):
    bn, nh, n, dh = q.shape
    rows = n // GRID_W
    kh = min(NA_KH, rows)
    scale = dh ** -0.5
    kg = k.reshape(bn, nh, rows, GRID_W, dh)
    vg = v.reshape(bn, nh, rows, GRID_W, dh)
    q_rows = q.reshape(bn, nh, rows, GRID_W, dh).transpose(2, 0, 1, 3, 4)
    col = jnp.arange(GRID_W)
    c0 = jnp.clip(col - NA_KW // 2, 0, GRID_W - NA_KW)
    col_mask = (col[None, :] >= c0[:, None]) & (col[None, :] < c0[:, None] + NA_KW)
    dc_idx = jnp.clip(col[None, :] - col[:, None] + NA_KW - 1, 0, 2 * NA_KW - 2)
    table_c = jnp.take(table, dc_idx, axis=2)
    band = kh * GRID_W

    def row_block(args):
        r, q_row = args
        r0 = jnp.clip(r - kh // 2, 0, rows - kh)
        k_band = lax.dynamic_slice_in_dim(kg, r0, kh, axis=2)
        v_band = lax.dynamic_slice_in_dim(vg, r0, kh, axis=2)
        dr_idx = r0 + jnp.arange(kh) - r + NA_KH - 1
        bias = jnp.take(table_c, dr_idx, axis=1).transpose(0, 2, 1, 3)
        s_lat = jnp.einsum("bhqd,bhrkd->bhqrk", q_row, k_band).astype(jnp.float32) * scale
        s_lat = s_lat + bias[None].astype(jnp.float32)
        s_lat = jnp.where(col_mask[:, None, :], s_lat, NEG_INF).reshape(bn, nh, GRID_W, band)
        s_ctx = jnp.einsum("bhqd,bhkd->bhqk", q_row, kc).astype(jnp.float32) * scale
        p = jax.nn.softmax(jnp.concatenate([s_lat, s_ctx], axis=-1), axis=-1).astype(q.dtype)
        o = jnp.einsum("bhqk,bhkd->bhqd", p[..., :band], v_band.reshape(bn, nh, band, dh))
        return o + jnp.einsum("bhqk,bhkd->bhqd", p[..., band:], vc)

    o = lax.map(row_block, (jnp.arange(rows), q_rows))
    return o.transpose(1, 2, 0, 3, 4).reshape(bn, nh, n, dh)


def hgrn_lower_bound(lb_logits, layer):
    p = jax.nn.softmax(lb_logits.astype(jnp.float32), axis=0)
    return jnp.cumsum(p, axis=0)[layer] - p[0]


def hgrn2_chunked(q, k, v, g, s0):
    bn, nh, n, dk = q.shape
    L = min(HG_CHUNK, n)
    nc = n // L
    chunks = lambda t: t.reshape(bn, nh, nc, L, t.shape[-1]).transpose(2, 0, 1, 3, 4)
    causal = jnp.tril(jnp.ones((L, L), dtype=bool))

    def step(s, inp):
        qc, kc, vc, gc = inp
        G = jnp.cumsum(gc, axis=2)
        o_inter = jnp.einsum("bhtd,bhdv->bhtv", qc * jnp.exp(G), s)
        diff = G[:, :, :, None, :] - G[:, :, None, :, :]
        decay = jnp.where(causal[:, :, None], jnp.exp(jnp.minimum(diff, 0.0)), 0.0)
        att = jnp.einsum("bhtd,bhsd,bhtsd->bhts", qc, kc, decay)
        o_intra = jnp.einsum("bhts,bhsv->bhtv", att, vc)
        g_end = G[:, :, -1:, :]
        s_new = jnp.exp(g_end[:, :, 0, :])[..., None] * s + jnp.einsum(
            "bhsd,bhsv->bhdv", kc * jnp.exp(g_end - G), vc)
        return s_new, o_inter + o_intra

    s_fin, o = lax.scan(step, s0, (chunks(q), chunks(k), chunks(v), chunks(g)))
    return s_fin, o.transpose(1, 2, 0, 3, 4).reshape(bn, nh, n, v.shape[-1])


def hgrn2_direction(zl, zc, f_idx, lb, reverse):
    def prep(z):
        q_raw, f_raw, i_raw = z[3], z[f_idx], z[6]
        if reverse:
            q_raw, f_raw, i_raw = q_raw[:, ::-1], f_raw[:, ::-1], i_raw[:, ::-1]
        f = lb + (1.0 - lb) * jax.nn.sigmoid(f_raw.astype(jnp.float32))
        return (split_heads(jax.nn.silu(q_raw.astype(jnp.float32)), HG_HEADS),
                split_heads(1.0 - f, HG_HEADS),
                split_heads(i_raw.astype(jnp.float32), HG_HEADS),
                split_heads(jnp.log(f), HG_HEADS))
    qc, kc, vc, gc = prep(zc)
    s0 = jnp.zeros((zc[3].shape[0], HG_HEADS, HG_HEAD_DIM, HG_HEAD_DIM), jnp.float32)
    s_ctx, o_c = hgrn2_chunked(qc, kc, vc, gc, s0)
    ql, kl, vl, gl = prep(zl)
    _, o_l = hgrn2_chunked(ql, kl, vl, gl, s_ctx)
    if reverse:
        o_l, o_c = o_l[:, :, ::-1], o_c[:, :, ::-1]
    return o_l, o_c


def cd_mixer(h, hc, w_in, w_out, table, lb_f, lb_b, gn_g, need_ctx):
    zl = jnp.split(h @ w_in, 8, axis=-1)
    zc = jnp.split(hc @ w_in, 8, axis=-1)
    qn, kn, vn = (split_heads(t, NA_HEADS) for t in zl[:3])
    qnc, knc, vnc = (split_heads(t, NA_HEADS) for t in zc[:3])
    o_na = merge_heads(na_latent(qn, kn, vn, knc, vnc, table))
    of_l, of_c = hgrn2_direction(zl, zc, 4, lb_f, False)
    ob_l, ob_c = hgrn2_direction(zl, zc, 5, lb_b, True)

    def hgrn_out(o, gate):
        return merge_heads(rmsnorm(o, gn_g)).astype(gate.dtype) * jax.nn.silu(gate)

    out_l = jnp.concatenate([o_na, hgrn_out(of_l + ob_l, zl[7])], axis=-1) @ w_out
    if not need_ctx:
        return out_l, None
    o_na_c = merge_heads(dense_attention(qnc, knc, vnc))
    out_c = jnp.concatenate([o_na_c, hgrn_out(of_c + ob_c, zc[7])], axis=-1) @ w_out
    return out_l, out_c


def ec_moe(h, router, w1, w3, w2):
    bn, n, _ = h.shape
    cap = EC_FACTOR * n // N_EXPERTS
    aff = jax.nn.softmax((h @ router).astype(jnp.float32), axis=-1)
    gate, idx = lax.top_k(jnp.swapaxes(aff, 1, 2), cap)
    bidx = jnp.arange(bn)[:, None, None]
    xs = h[bidx, idx]
    hid = jax.nn.silu(jnp.einsum("becd,edf->becf", xs, w1)) * jnp.einsum("becd,edf->becf", xs, w3)
    y = jnp.einsum("becf,efd->becd", hid, w2) * gate[..., None].astype(h.dtype)
    return jnp.zeros_like(h).at[bidx, idx].add(y)


def setup_inputs(seed: int = 0) -> dict:
    key = jax.random.key(seed)
    keys = jax.random.split(key, 32)
    f32 = jnp.float32
    D, G = D_MODEL, D_GROUP

    def normal(i, shape, scale):
        return jax.random.normal(keys[i], shape, f32) * scale

    lam_a = jax.random.uniform(keys[20], (N_AB, 2, G), f32, 0.9, 0.999)
    return {
        "x": normal(0, (BATCH, SEQ, D), 1.0),
        "c": normal(1, (BATCH, D), 1.0),
        "ctx": normal(2, (BATCH, CTX_LEN, D), 1.0),
        "c_ctx": normal(3, (D,), 1.0),
        "ada_w": normal(4, (DEPTH, D, 6 * D), 0.5 * D ** -0.5),
        "ada_b": normal(5, (DEPTH, 6 * D), 0.01),
        "norm_g": 1.0 + normal(6, (DEPTH, 2, D), 0.02),
        "final_g": 1.0 + normal(7, (D,), 0.02),
        "ab_w_in": normal(8, (N_AB, D, 4 * G), D ** -0.5),
        "ab_w_out": normal(9, (N_AB, 2 * G, D), (2 * G) ** -0.5),
        "conf_dw_w": normal(10, (N_AB, CONF_KERNEL, G), CONF_KERNEL ** -0.5),
        "conf_dw_b": normal(11, (N_AB, G), 0.01),
        "conf_ln_g": 1.0 + normal(12, (N_AB, G), 0.02),
        "conf_ln_b": normal(13, (N_AB, G), 0.01),
        "lru_conv_w": normal(14, (N_AB, LRU_CONV, G), LRU_CONV ** -0.5),
        "lru_conv_b": normal(15, (N_AB, G), 0.01),
        "lru_wa": normal(16, (N_AB, 2, LRU_BLOCKS, LRU_BLOCK_DIM, LRU_BLOCK_DIM), LRU_BLOCK_DIM ** -0.5),
        "lru_ba": normal(17, (N_AB, 2, G), 0.01),
        "lru_wi": normal(18, (N_AB, 2, LRU_BLOCKS, LRU_BLOCK_DIM, LRU_BLOCK_DIM), LRU_BLOCK_DIM ** -0.5),
        "lru_bi": normal(19, (N_AB, 2, G), 0.01),
        "lru_lam": jnp.log(lam_a) - jnp.log1p(-lam_a),
        "cd_w_in": normal(21, (N_CD, D, 8 * G), D ** -0.5),
        "cd_w_out": normal(22, (N_CD, 2 * G, D), (2 * G) ** -0.5),
        "na_bias": normal(23, (N_CD, NA_HEADS, 2 * NA_KH - 1, 2 * NA_KW - 1), 0.1),
        "hgrn_lb": normal(24, (2, DEPTH, G), 0.5),
        "hgrn_norm_g": 1.0 + normal(25, (N_CD, HG_HEAD_DIM), 0.02),
        "moe_router": normal(26, (DEPTH, D, N_EXPERTS), D ** -0.5),
        "moe_w1": normal(27, (DEPTH, N_EXPERTS, D, D_FF_EXPERT), D ** -0.5),
        "moe_w3": normal(28, (DEPTH, N_EXPERTS, D, D_FF_EXPERT), D ** -0.5),
        "moe_w2": normal(29, (DEPTH, N_EXPERTS, D_FF_EXPERT, D), D_FF_EXPERT ** -0.5),
    }


def reference(x, c, ctx, c_ctx, ada_w, ada_b, norm_g, final_g, ab_w_in, ab_w_out,
              conf_dw_w, conf_dw_b, conf_ln_g, conf_ln_b, lru_conv_w, lru_conv_b,
              lru_wa, lru_ba, lru_wi, lru_bi, lru_lam, cd_w_in, cd_w_out, na_bias,
              hgrn_lb, hgrn_norm_g, moe_router, moe_w1, moe_w3, moe_w2):
    xc = ctx
    s_lat = jax.nn.silu(c)
    s_ctx = jax.nn.silu(c_ctx)
    for l in range(DEPTH):
        last = l == DEPTH - 1
        j = l // 2
        mod_l = jnp.split((s_lat @ ada_w[l] + ada_b[l])[:, None, :], 6, axis=-1)
        mod_c = jnp.split(s_ctx @ ada_w[l] + ada_b[l], 6, axis=-1)
        h = modulate(rmsnorm(x, norm_g[l, 0]), mod_l[0], mod_l[1])
        hc = modulate(rmsnorm(xc, norm_g[l, 0]), mod_c[0], mod_c[1])
        if l % 2 == 0:
            o, oc = ab_mixer(h, hc, ab_w_in[j], ab_w_out[j], conf_dw_w[j], conf_dw_b[j],
                             conf_ln_g[j], conf_ln_b[j], lru_conv_w[j], lru_conv_b[j],
                             lru_wa[j], lru_ba[j], lru_wi[j], lru_bi[j], lru_lam[j], not last)
        else:
            lb_f = hgrn_lower_bound(hgrn_lb[0], l)
            lb_b = hgrn_lower_bound(hgrn_lb[1], l)
            o, oc = cd_mixer(h, hc, cd_w_in[j], cd_w_out[j], na_bias[j], lb_f, lb_b,
                             hgrn_norm_g[j], not last)
        x = x + mod_l[2] * o
        h = modulate(rmsnorm(x, norm_g[l, 1]), mod_l[3], mod_l[4])
        x = x + mod_l[5] * ec_moe(h, moe_router[l], moe_w1[l], moe_w3[l], moe_w2[l])
        if not last:
            xc = xc + mod_c[2] * oc
            hc = modulate(rmsnorm(xc, norm_g[l, 1]), mod_c[3], mod_c[4])
            xc = xc + mod_c[5] * ec_moe(hc, moe_router[l], moe_w1[l], moe_w3[l], moe_w2[l])
    return rmsnorm(x, final_g)
```

```python
from functools import partial

import jax
import jax.numpy as jnp
from jax import lax
from jax.experimental import pallas as pl
from jax.experimental.pallas import tpu as pltpu

D_MODEL = 2048
DEPTH = 2
GRID_W = 64
D_GROUP = D_MODEL // 2
CONF_KERNEL = 31
LRU_BLOCKS = 16
LRU_BLOCK_DIM = D_GROUP // LRU_BLOCKS
LRU_C = 8.0
NA_HEADS = 8
NA_KH = 8
NA_KW = 16
HG_HEADS = 8
HG_HEAD_DIM = D_GROUP // HG_HEADS
HG_CHUNK = 64
N_EXPERTS = 16
EC_FACTOR = 2
EPS = 1e-6
NEG_INF = -1e30

V7X_VMEM_LIMIT_BYTES = 56 * 1024 * 1024


def _matmul_kernel(x_ref, w_ref, o_ref, xb_ref):
    @pl.when(pl.program_id(1) == 0)
    def _():
        xb_ref[...] = x_ref[...].astype(jnp.bfloat16)

    o_ref[...] = jnp.dot(xb_ref[...], w_ref[...].astype(jnp.bfloat16),
                         preferred_element_type=jnp.float32)


def matmul(x, w, tm=1024, tn=512):
    m, k = x.shape
    n = w.shape[1]
    tm = min(tm, m)
    tn = min(tn, n)
    assert m % tm == 0 and n % tn == 0
    return pl.pallas_call(
        _matmul_kernel,
        grid=(m // tm, n // tn),
        in_specs=[pl.BlockSpec((tm, k), lambda i, j: (i, 0)),
                  pl.BlockSpec((k, tn), lambda i, j: (0, j))],
        out_specs=pl.BlockSpec((tm, tn), lambda i, j: (i, j)),
        out_shape=jax.ShapeDtypeStruct((m, n), jnp.float32),
        scratch_shapes=[pltpu.VMEM((tm, k), jnp.bfloat16)],
        compiler_params=pltpu.CompilerParams(
            dimension_semantics=("parallel", "arbitrary"),
            vmem_limit_bytes=V7X_VMEM_LIMIT_BYTES),
        name="dense_matmul",
    )(x, w)


def mm3(x, w):
    b, n, k = x.shape
    return matmul(x.reshape(b * n, k), w).reshape(b, n, w.shape[1])


def _ffn_kernel(xs_ref, w1_ref, w3_ref, w2_ref, g_ref, o_ref):
    f = pl.program_id(2)
    xs = xs_ref[...]
    a = jnp.dot(xs, w1_ref[...].astype(jnp.bfloat16), preferred_element_type=jnp.float32)
    b = jnp.dot(xs, w3_ref[...].astype(jnp.bfloat16), preferred_element_type=jnp.float32)
    hid = (a * jax.nn.sigmoid(a)) * b
    y = jnp.dot(hid.astype(jnp.bfloat16), w2_ref[...].astype(jnp.bfloat16),
                preferred_element_type=jnp.float32)

    @pl.when(f == 0)
    def _():
        o_ref[...] = y

    @pl.when(f != 0)
    def _():
        o_ref[...] += y

    @pl.when(f == pl.num_programs(2) - 1)
    def _():
        o_ref[...] = o_ref[...] * g_ref[...]


def expert_ffn(xs, w1, w3, w2, gate, tf=256):
    e, m, d = xs.shape
    f = w1.shape[2]
    tm = min(1024, m)
    assert m % tm == 0 and f % tf == 0
    return pl.pallas_call(
        _ffn_kernel,
        grid=(e, m // tm, f // tf),
        in_specs=[pl.BlockSpec((None, tm, d), lambda ei, mi, fi: (ei, mi, 0)),
                  pl.BlockSpec((None, d, tf), lambda ei, mi, fi: (ei, 0, fi)),
                  pl.BlockSpec((None, d, tf), lambda ei, mi, fi: (ei, 0, fi)),
                  pl.BlockSpec((None, tf, d), lambda ei, mi, fi: (ei, fi, 0)),
                  pl.BlockSpec((None, tm, 1), lambda ei, mi, fi: (ei, mi, 0))],
        out_specs=pl.BlockSpec((None, tm, d), lambda ei, mi, fi: (ei, mi, 0)),
        out_shape=jax.ShapeDtypeStruct((e, m, d), jnp.float32),
        compiler_params=pltpu.CompilerParams(
            dimension_semantics=("parallel", "parallel", "arbitrary"),
            vmem_limit_bytes=V7X_VMEM_LIMIT_BYTES),
        name="expert_ffn",
    )(xs, w1, w3, w2, gate)


LRU_SCAN_ROWS = 256


def _lru_scan_kernel(a_ref, b_ref, h0_ref, o_ref, carry_ref, *, reverse):
    tn = a_ref.shape[1]

    @pl.when(pl.program_id(0) == 0)
    def _():
        carry_ref[...] = h0_ref[...]

    def body(t, h):
        r = tn - 1 - t if reverse else t
        h = a_ref[:, pl.ds(r, 1), :] * h + b_ref[:, pl.ds(r, 1), :]
        o_ref[:, pl.ds(r, 1), :] = h
        return h

    carry_ref[...] = lax.fori_loop(0, tn, body, carry_ref[...], unroll=8)


def lru_scan(a, b, h0, reverse):
    bn, n, c = a.shape
    tn = min(LRU_SCAN_ROWS, n)
    nb = n // tn
    assert n % tn == 0
    idx = (lambda i: (0, nb - 1 - i, 0)) if reverse else (lambda i: (0, i, 0))
    return pl.pallas_call(
        partial(_lru_scan_kernel, reverse=reverse),
        grid=(nb,),
        in_specs=[pl.BlockSpec((bn, tn, c), idx),
                  pl.BlockSpec((bn, tn, c), idx),
                  pl.BlockSpec((bn, 1, c), lambda i: (0, 0, 0))],
        out_specs=pl.BlockSpec((bn, tn, c), idx),
        out_shape=jax.ShapeDtypeStruct((bn, n, c), jnp.float32),
        scratch_shapes=[pltpu.VMEM((bn, 1, c), jnp.float32)],
        compiler_params=pltpu.CompilerParams(
            dimension_semantics=("arbitrary",),
            vmem_limit_bytes=V7X_VMEM_LIMIT_BYTES),
        name="lru_scan_rev" if reverse else "lru_scan_fwd",
    )(a, b, h0[:, None, :])


def rmsnorm(x, g):
    y = x * lax.rsqrt(jnp.mean(x * x, axis=-1, keepdims=True) + EPS)
    return y * g


def layernorm(x, g, b):
    mu = jnp.mean(x, axis=-1, keepdims=True)
    var = jnp.mean(jnp.square(x - mu), axis=-1, keepdims=True)
    return (x - mu) * lax.rsqrt(var + EPS) * g + b


def modulate(h, shift, scale):
    return h * (1.0 + scale) + shift


def split_heads(t, n_heads):
    b, n, _ = t.shape
    return t.reshape(b, n, n_heads, -1).transpose(0, 2, 1, 3)


def merge_heads(t):
    b, h, n, d = t.shape
    return t.transpose(0, 2, 1, 3).reshape(b, n, h * d)


def depthwise_conv(x, w, b, pad_left, pad_right):
    y = lax.conv_general_dilated(
        x, w[:, None, :].astype(x.dtype), (1,), [(pad_left, pad_right)],
        dimension_numbers=("NWC", "WIO", "NWC"), feature_group_count=x.shape[-1])
    return y + b


def rglru_coeffs(x, w_a, b_a, w_i, b_i, lam):
    bn, n, _ = x.shape
    xb = x.reshape(bn, n, LRU_BLOCKS, LRU_BLOCK_DIM)
    r = jax.nn.sigmoid(jnp.einsum("bnhi,hij->bnhj", xb, w_a).reshape(bn, n, D_GROUP) + b_a)
    i = jax.nn.sigmoid(jnp.einsum("bnhi,hij->bnhj", xb, w_i).reshape(bn, n, D_GROUP) + b_i)
    log_a = (-LRU_C * r) * jax.nn.softplus(-lam)
    a = jnp.exp(log_a)
    b = jnp.sqrt(-jnp.expm1(2.0 * log_a)) * (i * x)
    return a, b


def rglru_direction(xl, xc, w_a, b_a, w_i, b_i, lam, reverse):
    ac, bc = rglru_coeffs(xc, w_a, b_a, w_i, b_i, lam)
    hc = lru_scan(ac, bc, jnp.zeros_like(bc[:, 0]), reverse)
    al, bl = rglru_coeffs(xl, w_a, b_a, w_i, b_i, lam)
    hl = lru_scan(al, bl, hc[:, 0] if reverse else hc[:, -1], reverse)
    return hl, hc


def ab_mixer(h, hc, w_in, w_out, dw_w, dw_b, ln_g, ln_b, cv_w, cv_b, wa, ba, wi, bi, lam, need_ctx):
    za, zg, zy, zx = jnp.split(h @ w_in, 4, axis=-1)
    ca, cg, cy, cx = jnp.split(mm3(hc, w_in), 4, axis=-1)

    def conformer(a_val, a_gate):
        u = a_val * jax.nn.sigmoid(a_gate)
        u = depthwise_conv(u, dw_w, dw_b, CONF_KERNEL // 2, CONF_KERNEL // 2)
        return jax.nn.silu(layernorm(u, ln_g, ln_b))

    xl = depthwise_conv(zx, cv_w, cv_b, 2, 1)
    xcc = depthwise_conv(cx, cv_w, cv_b, 2, 1)
    f_l, f_c = rglru_direction(xl, xcc, wa[0], ba[0], wi[0], bi[0], lam[0], False)
    b_l, b_c = rglru_direction(xl, xcc, wa[1], ba[1], wi[1], bi[1], lam[1], True)
    out_l = mm3(jnp.concatenate([conformer(za, zg), jax.nn.gelu(zy) * (f_l + b_l)], axis=-1), w_out)
    if not need_ctx:
        return out_l, None
    out_c = mm3(jnp.concatenate([conformer(ca, cg), jax.nn.gelu(cy) * (f_c + b_c)], axis=-1), w_out)
    return out_l, out_c


def dense_attention(q, k, v):
    s = jnp.einsum("bhqd,bhkd->bhqk", q, k) * (q.shape[-1] ** -0.5)
    return jnp.einsum("bhqk,bhkd->bhqd", jax.nn.softmax(s, axis=-1), v)


def na_latent(q, k, v, kc, vc, table):
    bn, nh, n, dh = q.shape
    rows = n // GRID_W
    kh = min(NA_KH, rows)
    scale = dh ** -0.5
    kg = k.reshape(bn, nh, rows, GRID_W, dh)
    vg = v.reshape(bn, nh, rows, GRID_W, dh)
    q_rows = q.reshape(bn, nh, rows, GRID_W, dh).transpose(2, 0, 1, 3, 4)
    col = jnp.arange(GRID_W)
    c0 = jnp.clip(col - NA_KW // 2, 0, GRID_W - NA_KW)
    col_mask = (col[None, :] >= c0[:, None]) & (col[None, :] < c0[:, None] + NA_KW)
    dc_idx = jnp.clip(col[None, :] - col[:, None] + NA_KW - 1, 0, 2 * NA_KW - 2)
    table_c = jnp.take(table, dc_idx, axis=2)
    band = kh * GRID_W

    def row_block(args):
        r, q_row = args
        r0 = jnp.clip(r - kh // 2, 0, rows - kh)
        k_band = lax.dynamic_slice_in_dim(kg, r0, kh, axis=2)
        v_band = lax.dynamic_slice_in_dim(vg, r0, kh, axis=2)
        dr_idx = r0 + jnp.arange(kh) - r + NA_KH - 1
        bias = jnp.take(table_c, dr_idx, axis=1).transpose(0, 2, 1, 3)
        s_lat = jnp.einsum("bhqd,bhrkd->bhqrk", q_row, k_band) * scale
        s_lat = s_lat + bias[None]
        s_lat = jnp.where(col_mask[:, None, :], s_lat, NEG_INF).reshape(bn, nh, GRID_W, band)
        s_ctx = jnp.einsum("bhqd,bhkd->bhqk", q_row, kc) * scale
        p = jax.nn.softmax(jnp.concatenate([s_lat, s_ctx], axis=-1), axis=-1)
        o = jnp.einsum("bhqk,bhkd->bhqd", p[..., :band], v_band.reshape(bn, nh, band, dh))
        return o + jnp.einsum("bhqk,bhkd->bhqd", p[..., band:], vc)

    o = lax.map(row_block, (jnp.arange(rows), q_rows))
    return o.transpose(1, 2, 0, 3, 4).reshape(bn, nh, n, dh)


def hgrn_lower_bound(lb_logits, layer):
    p = jax.nn.softmax(lb_logits, axis=0)
    return jnp.cumsum(p, axis=0)[layer] - p[0]


def hgrn2_chunked(q, k, v, g, s0):
    bn, nh, n, dk = q.shape
    L = min(HG_CHUNK, n)
    nc = n // L
    chunks = lambda t: t.reshape(bn, nh, nc, L, t.shape[-1]).transpose(2, 0, 1, 3, 4)
    causal = jnp.tril(jnp.ones((L, L), dtype=bool))

    def step(s, inp):
        qc, kc, vc, gc = inp
        G = jnp.cumsum(gc, axis=2)
        o_inter = jnp.einsum("bhtd,bhdv->bhtv", qc * jnp.exp(G), s)
        diff = G[:, :, :, None, :] - G[:, :, None, :, :]
        decay = jnp.where(causal[:, :, None], jnp.exp(jnp.minimum(diff, 0.0)), 0.0)
        att = jnp.einsum("bhtd,bhsd,bhtsd->bhts", qc, kc, decay)
        o_intra = jnp.einsum("bhts,bhsv->bhtv", att, vc)
        g_end = G[:, :, -1:, :]
        s_new = jnp.exp(g_end[:, :, 0, :])[..., None] * s + jnp.einsum(
            "bhsd,bhsv->bhdv", kc * jnp.exp(g_end - G), vc)
        return s_new, o_inter + o_intra

    s_fin, o = lax.scan(step, s0, (chunks(q), chunks(k), chunks(v), chunks(g)))
    return s_fin, o.transpose(1, 2, 0, 3, 4).reshape(bn, nh, n, v.shape[-1])


def hgrn2_direction(zl, zc, f_idx, lb, reverse):
    def prep(z):
        q_raw, f_raw, i_raw = z[3], z[f_idx], z[6]
        if reverse:
            q_raw, f_raw, i_raw = q_raw[:, ::-1], f_raw[:, ::-1], i_raw[:, ::-1]
        f = lb + (1.0 - lb) * jax.nn.sigmoid(f_raw)
        return (split_heads(jax.nn.silu(q_raw), HG_HEADS),
                split_heads(1.0 - f, HG_HEADS),
                split_heads(i_raw, HG_HEADS),
                split_heads(jnp.log(f), HG_HEADS))
    qc, kc, vc, gc = prep(zc)
    s0 = jnp.zeros((zc[3].shape[0], HG_HEADS, HG_HEAD_DIM, HG_HEAD_DIM), jnp.float32)
    s_ctx, o_c = hgrn2_chunked(qc, kc, vc, gc, s0)
    ql, kl, vl, gl = prep(zl)
    _, o_l = hgrn2_chunked(ql, kl, vl, gl, s_ctx)
    if reverse:
        o_l, o_c = o_l[:, :, ::-1], o_c[:, :, ::-1]
    return o_l, o_c


def cd_mixer(h, hc, w_in, w_out, table, lb_f, lb_b, gn_g, need_ctx):
    zl = jnp.split(mm3(h, w_in), 8, axis=-1)
    zc = jnp.split(mm3(hc, w_in), 8, axis=-1)
    qn, kn, vn = (split_heads(t, NA_HEADS) for t in zl[:3])
    qnc, knc, vnc = (split_heads(t, NA_HEADS) for t in zc[:3])
    o_na = merge_heads(na_latent(qn, kn, vn, knc, vnc, table))
    of_l, of_c = hgrn2_direction(zl, zc, 4, lb_f, False)
    ob_l, ob_c = hgrn2_direction(zl, zc, 5, lb_b, True)

    def hgrn_out(o, gate):
        return merge_heads(rmsnorm(o, gn_g)) * jax.nn.silu(gate)

    out_l = mm3(jnp.concatenate([o_na, hgrn_out(of_l + ob_l, zl[7])], axis=-1), w_out)
    if not need_ctx:
        return out_l, None
    o_na_c = merge_heads(dense_attention(qnc, knc, vnc))
    out_c = mm3(jnp.concatenate([o_na_c, hgrn_out(of_c + ob_c, zc[7])], axis=-1), w_out)
    return out_l, out_c


def ec_moe(h, router, w1, w3, w2):
    bn, n, d = h.shape
    cap = EC_FACTOR * n // N_EXPERTS
    logits = jnp.einsum("bnd,de->bne", h, router, precision=lax.Precision.HIGHEST)
    aff = jax.nn.softmax(logits, axis=-1)
    gate, idx = lax.top_k(jnp.swapaxes(aff, 1, 2), cap)
    bidx = jnp.arange(bn)[:, None, None]
    xs = h.astype(jnp.bfloat16)[bidx, idx]
    xs_e = xs.transpose(1, 0, 2, 3).reshape(N_EXPERTS, bn * cap, d)
    gate_e = gate.transpose(1, 0, 2).reshape(N_EXPERTS, bn * cap, 1)
    y = expert_ffn(xs_e, w1, w3, w2, gate_e)
    y = y.reshape(N_EXPERTS, bn, cap, d).transpose(1, 0, 2, 3)
    return jnp.zeros_like(h).at[bidx, idx].add(y)


def kernel(x, c, ctx, c_ctx, ada_w, ada_b, norm_g, final_g, ab_w_in, ab_w_out, conf_dw_w, conf_dw_b, conf_ln_g, conf_ln_b, lru_conv_w, lru_conv_b, lru_wa, lru_ba, lru_wi, lru_bi, lru_lam, cd_w_in, cd_w_out, na_bias, hgrn_lb, hgrn_norm_g, moe_router, moe_w1, moe_w3, moe_w2):
    xc = ctx
    s_lat = jax.nn.silu(c)
    s_ctx = jax.nn.silu(c_ctx)
    for l in range(DEPTH):
        last = l == DEPTH - 1
        j = l // 2
        mod_l = jnp.split((s_lat @ ada_w[l] + ada_b[l])[:, None, :], 6, axis=-1)
        mod_c = jnp.split(s_ctx @ ada_w[l] + ada_b[l], 6, axis=-1)
        h = modulate(rmsnorm(x, norm_g[l, 0]), mod_l[0], mod_l[1])
        hc = modulate(rmsnorm(xc, norm_g[l, 0]), mod_c[0], mod_c[1])
        if l % 2 == 0:
            o, oc = ab_mixer(h, hc, ab_w_in[j], ab_w_out[j], conf_dw_w[j], conf_dw_b[j],
                             conf_ln_g[j], conf_ln_b[j], lru_conv_w[j], lru_conv_b[j],
                             lru_wa[j], lru_ba[j], lru_wi[j], lru_bi[j], lru_lam[j], not last)
        else:
            lb_f = hgrn_lower_bound(hgrn_lb[0], l)
            lb_b = hgrn_lower_bound(hgrn_lb[1], l)
            o, oc = cd_mixer(h, hc, cd_w_in[j], cd_w_out[j], na_bias[j], lb_f, lb_b,
                             hgrn_norm_g[j], not last)
        x = x + mod_l[2] * o
        h = modulate(rmsnorm(x, norm_g[l, 1]), mod_l[3], mod_l[4])
        x = x + mod_l[5] * ec_moe(h, moe_router[l], moe_w1[l], moe_w3[l], moe_w2[l])
        if not last:
            xc = xc + mod_c[2] * oc
            hc = modulate(rmsnorm(xc, norm_g[l, 1]), mod_c[3], mod_c[4])
            xc = xc + mod_c[5] * ec_moe(hc, moe_router[l], moe_w1[l], moe_w3[l], moe_w2[l])
    return rmsnorm(x, final_g)
```

```python
from functools import partial

import jax
import jax.numpy as jnp
from jax import lax
from jax.experimental import pallas as pl
from jax.experimental.pallas import tpu as pltpu

D_MODEL = 2048
DEPTH = 2
GRID_W = 64
D_GROUP = D_MODEL // 2
CONF_KERNEL = 31
LRU_BLOCKS = 16
LRU_BLOCK_DIM = D_GROUP // LRU_BLOCKS
LRU_C = 8.0
NA_HEADS = 8
NA_KH = 8
NA_KW = 16
HG_HEADS = 8
HG_HEAD_DIM = D_GROUP // HG_HEADS
HG_CHUNK = 64
N_EXPERTS = 16
EC_FACTOR = 2
EPS = 1e-6
NEG_INF = -1e30

V7X_VMEM_LIMIT_BYTES = 56 * 1024 * 1024


def _matmul_kernel(x_ref, w_ref, o_ref, xb_ref):
    @pl.when(pl.program_id(1) == 0)
    def _():
        xb_ref[...] = x_ref[...].astype(jnp.bfloat16)

    o_ref[...] = jnp.dot(xb_ref[...], w_ref[...].astype(jnp.bfloat16),
                         preferred_element_type=jnp.float32)


def matmul(x, w, tm=1024, tn=512):
    m, k = x.shape
    n = w.shape[1]
    tm = min(tm, m)
    tn = min(tn, n)
    assert m % tm == 0 and n % tn == 0
    return pl.pallas_call(
        _matmul_kernel,
        grid=(m // tm, n // tn),
        in_specs=[pl.BlockSpec((tm, k), lambda i, j: (i, 0)),
                  pl.BlockSpec((k, tn), lambda i, j: (0, j))],
        out_specs=pl.BlockSpec((tm, tn), lambda i, j: (i, j)),
        out_shape=jax.ShapeDtypeStruct((m, n), jnp.float32),
        scratch_shapes=[pltpu.VMEM((tm, k), jnp.bfloat16)],
        compiler_params=pltpu.CompilerParams(
            dimension_semantics=("parallel", "arbitrary"),
            vmem_limit_bytes=V7X_VMEM_LIMIT_BYTES),
        name="dense_matmul",
    )(x, w)


def mm3(x, w):
    b, n, k = x.shape
    return matmul(x.reshape(b * n, k), w).reshape(b, n, w.shape[1])


def _ffn_kernel(xs_ref, w1_ref, w3_ref, w2_ref, g_ref, o_ref):
    f = pl.program_id(2)
    xs = xs_ref[...]
    a = jnp.dot(xs, w1_ref[...].astype(jnp.bfloat16), preferred_element_type=jnp.float32)
    b = jnp.dot(xs, w3_ref[...].astype(jnp.bfloat16), preferred_element_type=jnp.float32)
    hid = (a * jax.nn.sigmoid(a)) * b
    y = jnp.dot(hid.astype(jnp.bfloat16), w2_ref[...].astype(jnp.bfloat16),
                preferred_element_type=jnp.float32)

    @pl.when(f == 0)
    def _():
        o_ref[...] = y

    @pl.when(f != 0)
    def _():
        o_ref[...] += y

    @pl.when(f == pl.num_programs(2) - 1)
    def _():
        o_ref[...] = o_ref[...] * g_ref[...]


def expert_ffn(xs, w1, w3, w2, gate, tf=256):
    e, m, d = xs.shape
    f = w1.shape[2]
    tm = min(1024, m)
    assert m % tm == 0 and f % tf == 0
    return pl.pallas_call(
        _ffn_kernel,
        grid=(e, m // tm, f // tf),
        in_specs=[pl.BlockSpec((None, tm, d), lambda ei, mi, fi: (ei, mi, 0)),
                  pl.BlockSpec((None, d, tf), lambda ei, mi, fi: (ei, 0, fi)),
                  pl.BlockSpec((None, d, tf), lambda ei, mi, fi: (ei, 0, fi)),
                  pl.BlockSpec((None, tf, d), lambda ei, mi, fi: (ei, fi, 0)),
                  pl.BlockSpec((None, tm, 1), lambda ei, mi, fi: (ei, mi, 0))],
        out_specs=pl.BlockSpec((None, tm, d), lambda ei, mi, fi: (ei, mi, 0)),
        out_shape=jax.ShapeDtypeStruct((e, m, d), jnp.float32),
        compiler_params=pltpu.CompilerParams(
            dimension_semantics=("parallel", "parallel", "arbitrary"),
            vmem_limit_bytes=V7X_VMEM_LIMIT_BYTES),
        name="expert_ffn",
    )(xs, w1, w3, w2, gate)


LRU_SCAN_ROWS = 256


def _lru_scan_kernel(a_ref, b_ref, h0_ref, o_ref, carry_ref, *, reverse):
    tn = a_ref.shape[1]

    @pl.when(pl.program_id(0) == 0)
    def _():
        carry_ref[...] = h0_ref[...]

    def body(t, h):
        r = tn - 1 - t if reverse else t
        h = a_ref[:, pl.ds(r, 1), :] * h + b_ref[:, pl.ds(r, 1), :]
        o_ref[:, pl.ds(r, 1), :] = h
        return h

    carry_ref[...] = lax.fori_loop(0, tn, body, carry_ref[...], unroll=8)


def lru_scan(a, b, h0, reverse):
    bn, n, c = a.shape
    tn = min(LRU_SCAN_ROWS, n)
    nb = n // tn
    assert n % tn == 0
    idx = (lambda i: (0, nb - 1 - i, 0)) if reverse else (lambda i: (0, i, 0))
    return pl.pallas_call(
        partial(_lru_scan_kernel, reverse=reverse),
        grid=(nb,),
        in_specs=[pl.BlockSpec((bn, tn, c), idx),
                  pl.BlockSpec((bn, tn, c), idx),
                  pl.BlockSpec((bn, 1, c), lambda i: (0, 0, 0))],
        out_specs=pl.BlockSpec((bn, tn, c), idx),
        out_shape=jax.ShapeDtypeStruct((bn, n, c), jnp.float32),
        scratch_shapes=[pltpu.VMEM((bn, 1, c), jnp.float32)],
        compiler_params=pltpu.CompilerParams(
            dimension_semantics=("arbitrary",),
            vmem_limit_bytes=V7X_VMEM_LIMIT_BYTES),
        name="lru_scan_rev" if reverse else "lru_scan_fwd",
    )(a, b, h0[:, None, :])


HG_SUB = 16
HG_ROWS = 512


def _hgrn2_kernel(*refs, reverse, epilogue):
    if epilogue:
        (q_ref, f_ref, v_ref, gate_ref, of_ref, s0_ref, lb_ref, gn_ref,
         o_ref, sfin_ref, st_ref, q_s, k_s, g_s, o_s) = refs
    else:
        (q_ref, f_ref, v_ref, s0_ref, lb_ref,
         o_ref, sfin_ref, st_ref, q_s, k_s, g_s) = refs
        o_s = o_ref
    tb = q_ref.shape[0]
    ti = pl.program_id(2)

    @pl.when(ti == 0)
    def _():
        st_ref[...] = s0_ref[...]

    lb = lb_ref[...]
    f = lb + (1.0 - lb) * jax.nn.sigmoid(f_ref[...])
    k_s[...] = 1.0 - f
    q = q_ref[...]
    q_s[...] = q * jax.nn.sigmoid(q)
    g = jnp.log(f)
    pos = lax.broadcasted_iota(jnp.int32, (tb, 1), 0) % HG_SUB
    sh = 1
    while sh < HG_SUB:
        if reverse:
            g = g + jnp.where(pos < HG_SUB - sh, pltpu.roll(g, tb - sh, 0), 0.0)
        else:
            g = g + jnp.where(pos >= sh, pltpu.roll(g, sh, 0), 0.0)
        sh *= 2
    g_s[...] = g

    row = lax.broadcasted_iota(jnp.int32, (HG_SUB, 1), 0)
    last = 0 if reverse else HG_SUB - 1
    nchunk = tb // HG_SUB

    def chunk(ci, carry):
        c = nchunk - 1 - ci if reverse else ci
        rows = pl.ds(pl.multiple_of(c * HG_SUB, HG_SUB), HG_SUB)
        qc, kc, gc, vc = q_s[rows, :], k_s[rows, :], g_s[rows, :], v_ref[rows, :]
        gt = gc[last:last + 1, :]
        st = st_ref[...]
        qs = qc * jnp.exp(gc)
        o = lax.dot_general(qs.astype(jnp.bfloat16), st.astype(jnp.bfloat16),
                            (((1,), (1,)), ((), ())), preferred_element_type=jnp.float32)
        for t in range(HG_SUB):
            e = jnp.exp(jnp.minimum(gc[t:t + 1, :] - gc, 0.0))
            a = jnp.sum(kc * e * qc[t:t + 1, :], axis=1, keepdims=True)
            a = jnp.where((row >= t) if reverse else (row <= t), a, 0.0)
            ot = jnp.sum(a * vc, axis=0, keepdims=True)
            o = o + jnp.where(row == t, ot, 0.0)
        o_s[rows, :] = o
        ks = kc * jnp.exp(gt - gc)
        upd = lax.dot_general(vc.astype(jnp.bfloat16), ks.astype(jnp.bfloat16),
                              (((0,), (0,)), ((), ())), preferred_element_type=jnp.float32)
        st_ref[...] = st * jnp.exp(gt) + upd
        return carry

    lax.fori_loop(0, nchunk, chunk, 0)

    if epilogue:
        o = o_s[...] + of_ref[...]
        y = o * lax.rsqrt(jnp.mean(o * o, axis=-1, keepdims=True) + EPS) * gn_ref[...]
        gate = gate_ref[...]
        o_ref[...] = y * (gate * jax.nn.sigmoid(gate))

    @pl.when(ti == pl.num_programs(2) - 1)
    def _():
        sfin_ref[...] = st_ref[...]


def hgrn2_scan(z, f_col, s0, lb, reverse, of=None, gn_g=None):
    bn, n, _ = z.shape
    nh, hd = HG_HEADS, HG_HEAD_DIM
    tb = min(HG_ROWS, n)
    nb = n // tb
    assert n % tb == 0 and tb % HG_SUB == 0
    epilogue = of is not None
    tix = (lambda t: nb - 1 - t) if reverse else (lambda t: t)

    def col(group):
        return pl.BlockSpec((None, tb, hd), lambda b, h, t: (b, tix(t), group * nh + h))

    o_spec = pl.BlockSpec((None, tb, hd), lambda b, h, t: (b, tix(t), h))
    s_spec = pl.BlockSpec((None, None, hd, hd), lambda b, h, t: (b, h, 0, 0))
    lb_spec = pl.BlockSpec((None, 1, hd), lambda b, h, t: (h, 0, 0))
    gn_spec = pl.BlockSpec((1, hd), lambda b, h, t: (0, 0))
    lb3 = lb.reshape(nh, 1, hd)
    scratch = [pltpu.VMEM((hd, hd), jnp.float32)] + [pltpu.VMEM((tb, hd), jnp.float32)] * 3
    if epilogue:
        in_specs = [col(3), col(f_col), col(6), col(7), o_spec, s_spec, lb_spec, gn_spec]
        args = (z, z, z, z, of, s0, lb3, gn_g.reshape(1, hd))
        scratch = scratch + [pltpu.VMEM((tb, hd), jnp.float32)]
    else:
        in_specs = [col(3), col(f_col), col(6), s_spec, lb_spec]
        args = (z, z, z, s0, lb3)
    return pl.pallas_call(
        partial(_hgrn2_kernel, reverse=reverse, epilogue=epilogue),
        grid=(bn, nh, nb),
        in_specs=in_specs,
        out_specs=[o_spec, s_spec],
        out_shape=[jax.ShapeDtypeStruct((bn, n, nh * hd), jnp.float32),
                   jax.ShapeDtypeStruct((bn, nh, hd, hd), jnp.float32)],
        scratch_shapes=scratch,
        compiler_params=pltpu.CompilerParams(
            dimension_semantics=("parallel", "parallel", "arbitrary")),
        name="hgrn2_" + ("rev" if reverse else "fwd") + ("_out" if epilogue else ""),
    )(*args)


NA_QROWS = 4
NA_UROWS = 12


def _na_kernel(q_ref, k_ref, v_ref, kc_ref, vc_ref, bias_ref, o_ref, *, rows):
    tq = NA_QROWS * GRID_W
    tk = NA_UROWS * GRID_W
    ngroups = rows // NA_QROWS
    scale = q_ref.shape[1] ** -0.5
    kc = kc_ref[...].astype(jnp.bfloat16)
    vc = vc_ref[...].astype(jnp.bfloat16)
    nt = (((1,), (1,)), ((), ()))

    def group(g, carry):
        q = q_ref[pl.ds(pl.multiple_of(g * tq, tq), tq), :].astype(jnp.bfloat16)
        u0 = jnp.clip(g * NA_QROWS - NA_KH // 2, 0, rows - NA_UROWS)
        krows = pl.ds(pl.multiple_of(u0 * GRID_W, GRID_W), tk)
        k_u = k_ref[krows, :].astype(jnp.bfloat16)
        v_u = v_ref[krows, :].astype(jnp.bfloat16)
        cls = jnp.where(g == 0, 0, jnp.where(g == ngroups - 1, 2, 1))
        s = lax.dot_general(q, k_u, nt, preferred_element_type=jnp.float32) * scale + bias_ref[cls]
        sc = lax.dot_general(q, kc, nt, preferred_element_type=jnp.float32) * scale
        m = jnp.maximum(jnp.max(s, axis=-1, keepdims=True), jnp.max(sc, axis=-1, keepdims=True))
        p = jnp.exp(s - m)
        pc = jnp.exp(sc - m)
        denom = jnp.sum(p, axis=-1, keepdims=True) + jnp.sum(pc, axis=-1, keepdims=True)
        o = (jnp.dot(p.astype(jnp.bfloat16), v_u, preferred_element_type=jnp.float32)
             + jnp.dot(pc.astype(jnp.bfloat16), vc, preferred_element_type=jnp.float32))
        o_ref[pl.ds(pl.multiple_of(g * tq, tq), tq), :] = o / denom
        return carry

    lax.fori_loop(0, ngroups, group, 0)


def na_bias_classes(table, rows):
    kh = min(NA_KH, rows)
    col = jnp.arange(GRID_W)
    c0 = jnp.clip(col - NA_KW // 2, 0, GRID_W - NA_KW)
    col_mask = (col[None, :] >= c0[:, None]) & (col[None, :] < c0[:, None] + NA_KW)
    dc_idx = jnp.clip(col[None, :] - col[:, None] + NA_KW - 1, 0, 2 * NA_KW - 2)
    table_c = jnp.take(table, dc_idx, axis=2)
    ngroups = rows // NA_QROWS
    out = []
    for g in (0, 1, ngroups - 1):
        rq = g * NA_QROWS + jnp.arange(NA_QROWS)
        r0 = jnp.clip(rq - kh // 2, 0, rows - kh)
        u0 = jnp.clip(g * NA_QROWS - NA_KH // 2, 0, rows - NA_UROWS)
        key_row = u0 + jnp.arange(NA_UROWS)
        valid = (key_row[None, :] >= r0[:, None]) & (key_row[None, :] < r0[:, None] + kh)
        dr = jnp.clip(key_row[None, :] - rq[:, None] + NA_KH - 1, 0, 2 * NA_KH - 2)
        b = table_c[:, dr]
        ok = valid[None, :, :, None, None] & col_mask[None, None, None, :, :]
        b = jnp.where(ok, b, NEG_INF).transpose(0, 1, 3, 2, 4)
        out.append(b.reshape(table.shape[0], NA_QROWS * GRID_W, NA_UROWS * GRID_W))
    return jnp.stack(out, axis=1)


def na_attention(z_l, z_c, table):
    bn, n, _ = z_l.shape
    nc = z_c.shape[1]
    nh = NA_HEADS
    dh = D_GROUP // nh
    rows = n // GRID_W
    assert rows % NA_QROWS == 0 and rows >= NA_UROWS and rows // NA_QROWS >= 3
    bias = na_bias_classes(table, rows)

    def col(nrow, group):
        return pl.BlockSpec((None, nrow, dh), lambda b, h: (b, 0, group * nh + h))

    return pl.pallas_call(
        partial(_na_kernel, rows=rows),
        grid=(bn, nh),
        in_specs=[col(n, 0), col(n, 1), col(n, 2), col(nc, 1), col(nc, 2),
                  pl.BlockSpec((None, 3, NA_QROWS * GRID_W, NA_UROWS * GRID_W), lambda b, h: (h, 0, 0, 0))],
        out_specs=pl.BlockSpec((None, n, dh), lambda b, h: (b, 0, h)),
        out_shape=jax.ShapeDtypeStruct((bn, n, nh * dh), jnp.float32),
        compiler_params=pltpu.CompilerParams(
            dimension_semantics=("parallel", "parallel"),
            vmem_limit_bytes=V7X_VMEM_LIMIT_BYTES),
        name="na_attention",
    )(z_l, z_l, z_l, z_c, z_c, bias)


def rmsnorm(x, g):
    y = x * lax.rsqrt(jnp.mean(x * x, axis=-1, keepdims=True) + EPS)
    return y * g


def layernorm(x, g, b):
    mu = jnp.mean(x, axis=-1, keepdims=True)
    var = jnp.mean(jnp.square(x - mu), axis=-1, keepdims=True)
    return (x - mu) * lax.rsqrt(var + EPS) * g + b


def modulate(h, shift, scale):
    return h * (1.0 + scale) + shift


def split_heads(t, n_heads):
    b, n, _ = t.shape
    return t.reshape(b, n, n_heads, -1).transpose(0, 2, 1, 3)


def merge_heads(t):
    b, h, n, d = t.shape
    return t.transpose(0, 2, 1, 3).reshape(b, n, h * d)


def depthwise_conv(x, w, b, pad_left, pad_right):
    y = lax.conv_general_dilated(
        x, w[:, None, :].astype(x.dtype), (1,), [(pad_left, pad_right)],
        dimension_numbers=("NWC", "WIO", "NWC"), feature_group_count=x.shape[-1])
    return y + b


def rglru_coeffs(x, w_a, b_a, w_i, b_i, lam):
    bn, n, _ = x.shape
    xb = x.reshape(bn, n, LRU_BLOCKS, LRU_BLOCK_DIM)
    r = jax.nn.sigmoid(jnp.einsum("bnhi,hij->bnhj", xb, w_a).reshape(bn, n, D_GROUP) + b_a)
    i = jax.nn.sigmoid(jnp.einsum("bnhi,hij->bnhj", xb, w_i).reshape(bn, n, D_GROUP) + b_i)
    log_a = (-LRU_C * r) * jax.nn.softplus(-lam)
    a = jnp.exp(log_a)
    b = jnp.sqrt(-jnp.expm1(2.0 * log_a)) * (i * x)
    return a, b


def rglru_direction(xl, xc, w_a, b_a, w_i, b_i, lam, reverse):
    ac, bc = rglru_coeffs(xc, w_a, b_a, w_i, b_i, lam)
    hc = lru_scan(ac, bc, jnp.zeros_like(bc[:, 0]), reverse)
    al, bl = rglru_coeffs(xl, w_a, b_a, w_i, b_i, lam)
    hl = lru_scan(al, bl, hc[:, 0] if reverse else hc[:, -1], reverse)
    return hl, hc


def ab_mixer(h, hc, w_in, w_out, dw_w, dw_b, ln_g, ln_b, cv_w, cv_b, wa, ba, wi, bi, lam, need_ctx):
    za, zg, zy, zx = jnp.split(h @ w_in, 4, axis=-1)
    ca, cg, cy, cx = jnp.split(mm3(hc, w_in), 4, axis=-1)

    def conformer(a_val, a_gate):
        u = a_val * jax.nn.sigmoid(a_gate)
        u = depthwise_conv(u, dw_w, dw_b, CONF_KERNEL // 2, CONF_KERNEL // 2)
        return jax.nn.silu(layernorm(u, ln_g, ln_b))

    xl = depthwise_conv(zx, cv_w, cv_b, 2, 1)
    xcc = depthwise_conv(cx, cv_w, cv_b, 2, 1)
    f_l, f_c = rglru_direction(xl, xcc, wa[0], ba[0], wi[0], bi[0], lam[0], False)
    b_l, b_c = rglru_direction(xl, xcc, wa[1], ba[1], wi[1], bi[1], lam[1], True)
    out_l = mm3(jnp.concatenate([conformer(za, zg), jax.nn.gelu(zy) * (f_l + b_l)], axis=-1), w_out)
    if not need_ctx:
        return out_l, None
    out_c = mm3(jnp.concatenate([conformer(ca, cg), jax.nn.gelu(cy) * (f_c + b_c)], axis=-1), w_out)
    return out_l, out_c


def dense_attention(q, k, v):
    s = jnp.einsum("bhqd,bhkd->bhqk", q, k) * (q.shape[-1] ** -0.5)
    return jnp.einsum("bhqk,bhkd->bhqd", jax.nn.softmax(s, axis=-1), v)


def hgrn_lower_bound(lb_logits, layer):
    p = jax.nn.softmax(lb_logits, axis=0)
    return jnp.cumsum(p, axis=0)[layer] - p[0]


def cd_mixer(h, hc, w_in, w_out, table, lb_f, lb_b, gn_g, need_ctx):
    z_l = mm3(h, w_in)
    z_c = mm3(hc, w_in)
    o_na = na_attention(z_l, z_c, table)
    s0 = jnp.zeros((h.shape[0], HG_HEADS, HG_HEAD_DIM, HG_HEAD_DIM), jnp.float32)
    of_c, s_f = hgrn2_scan(z_c, 4, s0, lb_f, False)
    of_l, _ = hgrn2_scan(z_l, 4, s_f, lb_f, False)
    if need_ctx:
        ho_c, s_b = hgrn2_scan(z_c, 5, s0, lb_b, True, of=of_c, gn_g=gn_g)
    else:
        _, s_b = hgrn2_scan(z_c, 5, s0, lb_b, True)
    ho_l, _ = hgrn2_scan(z_l, 5, s_b, lb_b, True, of=of_l, gn_g=gn_g)
    out_l = mm3(jnp.concatenate([o_na, ho_l], axis=-1), w_out)
    if not need_ctx:
        return out_l, None
    qnc, knc, vnc = (split_heads(t, NA_HEADS) for t in jnp.split(z_c, 8, axis=-1)[:3])
    o_na_c = merge_heads(dense_attention(qnc, knc, vnc))
    out_c = mm3(jnp.concatenate([o_na_c, ho_c], axis=-1), w_out)
    return out_l, out_c


def ec_moe(h, router, w1, w3, w2):
    bn, n, d = h.shape
    cap = EC_FACTOR * n // N_EXPERTS
    logits = jnp.einsum("bnd,de->bne", h, router, precision=lax.Precision.HIGHEST)
    aff = jax.nn.softmax(logits, axis=-1)
    gate, idx = lax.top_k(jnp.swapaxes(aff, 1, 2), cap)
    bidx = jnp.arange(bn)[:, None, None]
    xs = h.astype(jnp.bfloat16)[bidx, idx]
    xs_e = xs.transpose(1, 0, 2, 3).reshape(N_EXPERTS, bn * cap, d)
    gate_e = gate.transpose(1, 0, 2).reshape(N_EXPERTS, bn * cap, 1)
    y = expert_ffn(xs_e, w1, w3, w2, gate_e)
    y = y.reshape(N_EXPERTS, bn, cap, d).transpose(1, 0, 2, 3)
    return jnp.zeros_like(h).at[bidx, idx].add(y)


def kernel(x, c, ctx, c_ctx, ada_w, ada_b, norm_g, final_g, ab_w_in, ab_w_out, conf_dw_w, conf_dw_b, conf_ln_g, conf_ln_b, lru_conv_w, lru_conv_b, lru_wa, lru_ba, lru_wi, lru_bi, lru_lam, cd_w_in, cd_w_out, na_bias, hgrn_lb, hgrn_norm_g, moe_router, moe_w1, moe_w3, moe_w2):
    xc = ctx
    s_lat = jax.nn.silu(c)
    s_ctx = jax.nn.silu(c_ctx)
    for l in range(DEPTH):
        last = l == DEPTH - 1
        j = l // 2
        mod_l = jnp.split((s_lat @ ada_w[l] + ada_b[l])[:, None, :], 6, axis=-1)
        mod_c = jnp.split(s_ctx @ ada_w[l] + ada_b[l], 6, axis=-1)
        h = modulate(rmsnorm(x, norm_g[l, 0]), mod_l[0], mod_l[1])
        hc = modulate(rmsnorm(xc, norm_g[l, 0]), mod_c[0], mod_c[1])
        if l % 2 == 0:
            o, oc = ab_mixer(h, hc, ab_w_in[j], ab_w_out[j], conf_dw_w[j], conf_dw_b[j],
                             conf_ln_g[j], conf_ln_b[j], lru_conv_w[j], lru_conv_b[j],
                             lru_wa[j], lru_ba[j], lru_wi[j], lru_bi[j], lru_lam[j], not last)
        else:
            lb_f = hgrn_lower_bound(hgrn_lb[0], l)
            lb_b = hgrn_lower_bound(hgrn_lb[1], l)
            o, oc = cd_mixer(h, hc, cd_w_in[j], cd_w_out[j], na_bias[j], lb_f, lb_b,
                             hgrn_norm_g[j], not last)
        x = x + mod_l[2] * o
        h = modulate(rmsnorm(x, norm_g[l, 1]), mod_l[3], mod_l[4])
        x = x + mod_l[5] * ec_moe(h, moe_router[l], moe_w1[l], moe_w3[l], moe_w2[l])
        if not last:
            xc = xc + mod_c[2] * oc
            hc = modulate(rmsnorm(xc, norm_g[l, 1]), mod_c[3], mod_c[4])
            xc = xc + mod_c[5] * ec_moe(hc, moe_router[l], moe_w1[l], moe_w3[l], moe_w2[l])
    return rmsnorm(x, final_g)
```

```python
from functools import partial

import jax
import jax.numpy as jnp
from jax import lax
from jax.experimental import pallas as pl
from jax.experimental.pallas import tpu as pltpu

D_MODEL = 2048
DEPTH = 2
GRID_W = 64
D_GROUP = D_MODEL // 2
CONF_KERNEL = 31
LRU_BLOCKS = 16
LRU_BLOCK_DIM = D_GROUP // LRU_BLOCKS
LRU_C = 8.0
NA_HEADS = 8
NA_KH = 8
NA_KW = 16
HG_HEADS = 8
HG_HEAD_DIM = D_GROUP // HG_HEADS
HG_CHUNK = 64
N_EXPERTS = 16
EC_FACTOR = 2
EPS = 1e-6
NEG_INF = -1e30

V7X_VMEM_LIMIT_BYTES = 56 * 1024 * 1024


def _matmul_kernel(x_ref, w_ref, o_ref, xb_ref):
    @pl.when(pl.program_id(1) == 0)
    def _():
        xb_ref[...] = x_ref[...].astype(jnp.bfloat16)

    o_ref[...] = jnp.dot(xb_ref[...], w_ref[...].astype(jnp.bfloat16),
                         preferred_element_type=jnp.float32)


def matmul(x, w, tm=1024, tn=512):
    m, k = x.shape
    n = w.shape[1]
    tm = min(tm, m)
    tn = min(tn, n)
    assert m % tm == 0 and n % tn == 0
    return pl.pallas_call(
        _matmul_kernel,
        grid=(m // tm, n // tn),
        in_specs=[pl.BlockSpec((tm, k), lambda i, j: (i, 0)),
                  pl.BlockSpec((k, tn), lambda i, j: (0, j))],
        out_specs=pl.BlockSpec((tm, tn), lambda i, j: (i, j)),
        out_shape=jax.ShapeDtypeStruct((m, n), jnp.float32),
        scratch_shapes=[pltpu.VMEM((tm, k), jnp.bfloat16)],
        compiler_params=pltpu.CompilerParams(
            dimension_semantics=("parallel", "arbitrary"),
            vmem_limit_bytes=V7X_VMEM_LIMIT_BYTES),
        name="dense_matmul",
    )(x, w)


def mm3(x, w):
    b, n, k = x.shape
    return matmul(x.reshape(b * n, k), w).reshape(b, n, w.shape[1])


def _ffn_kernel(xs_ref, w1_ref, w3_ref, w2_ref, g_ref, o_ref):
    f = pl.program_id(2)
    xs = xs_ref[...]
    a = jnp.dot(xs, w1_ref[...].astype(jnp.bfloat16), preferred_element_type=jnp.float32)
    b = jnp.dot(xs, w3_ref[...].astype(jnp.bfloat16), preferred_element_type=jnp.float32)
    hid = (a * jax.nn.sigmoid(a)) * b
    y = jnp.dot(hid.astype(jnp.bfloat16), w2_ref[...].astype(jnp.bfloat16),
                preferred_element_type=jnp.float32)

    @pl.when(f == 0)
    def _():
        o_ref[...] = y

    @pl.when(f != 0)
    def _():
        o_ref[...] += y

    @pl.when(f == pl.num_programs(2) - 1)
    def _():
        o_ref[...] = o_ref[...] * g_ref[...]


def expert_ffn(xs, w1, w3, w2, gate, layer, tf=256):
    e, m, d = xs.shape
    f = w1.shape[3]
    tm = min(1024, m)
    assert m % tm == 0 and f % tf == 0
    return pl.pallas_call(
        _ffn_kernel,
        grid=(e, m // tm, f // tf),
        in_specs=[pl.BlockSpec((None, tm, d), lambda ei, mi, fi: (ei, mi, 0)),
                  pl.BlockSpec((None, None, d, tf), lambda ei, mi, fi: (layer, ei, 0, fi)),
                  pl.BlockSpec((None, None, d, tf), lambda ei, mi, fi: (layer, ei, 0, fi)),
                  pl.BlockSpec((None, None, tf, d), lambda ei, mi, fi: (layer, ei, fi, 0)),
                  pl.BlockSpec((None, tm, 1), lambda ei, mi, fi: (ei, mi, 0))],
        out_specs=pl.BlockSpec((None, tm, d), lambda ei, mi, fi: (ei, mi, 0)),
        out_shape=jax.ShapeDtypeStruct((e, m, d), jnp.float32),
        compiler_params=pltpu.CompilerParams(
            dimension_semantics=("parallel", "parallel", "arbitrary"),
            vmem_limit_bytes=V7X_VMEM_LIMIT_BYTES),
        name="expert_ffn",
    )(xs, w1, w3, w2, gate)


LRU_SCAN_ROWS = 256


def _lru_scan_kernel(a_ref, b_ref, h0_ref, o_ref, carry_ref, *, reverse):
    tn = a_ref.shape[1]

    @pl.when(pl.program_id(0) == 0)
    def _():
        carry_ref[...] = h0_ref[...]

    def body(t, h):
        r = tn - 1 - t if reverse else t
        h = a_ref[:, pl.ds(r, 1), :] * h + b_ref[:, pl.ds(r, 1), :]
        o_ref[:, pl.ds(r, 1), :] = h
        return h

    carry_ref[...] = lax.fori_loop(0, tn, body, carry_ref[...], unroll=8)


def lru_scan(a, b, h0, reverse):
    bn, n, c = a.shape
    tn = min(LRU_SCAN_ROWS, n)
    nb = n // tn
    assert n % tn == 0
    idx = (lambda i: (0, nb - 1 - i, 0)) if reverse else (lambda i: (0, i, 0))
    return pl.pallas_call(
        partial(_lru_scan_kernel, reverse=reverse),
        grid=(nb,),
        in_specs=[pl.BlockSpec((bn, tn, c), idx),
                  pl.BlockSpec((bn, tn, c), idx),
                  pl.BlockSpec((bn, 1, c), lambda i: (0, 0, 0))],
        out_specs=pl.BlockSpec((bn, tn, c), idx),
        out_shape=jax.ShapeDtypeStruct((bn, n, c), jnp.float32),
        scratch_shapes=[pltpu.VMEM((bn, 1, c), jnp.float32)],
        compiler_params=pltpu.CompilerParams(
            dimension_semantics=("arbitrary",),
            vmem_limit_bytes=V7X_VMEM_LIMIT_BYTES),
        name="lru_scan_rev" if reverse else "lru_scan_fwd",
    )(a, b, h0[:, None, :])


HG_SUB = 32
HG_ROWS = 512


def _hgrn2_kernel(*refs, reverse, epilogue):
    if epilogue:
        (q_ref, f_ref, v_ref, gate_ref, of_ref, s0_ref, lb_ref, gn_ref,
         o_ref, sfin_ref, st_ref, q_s, k_s, g_s, o_s) = refs
    else:
        (q_ref, f_ref, v_ref, s0_ref, lb_ref,
         o_ref, sfin_ref, st_ref, q_s, k_s, g_s) = refs
        o_s = o_ref
    tb = q_ref.shape[0]
    ti = pl.program_id(2)

    @pl.when(ti == 0)
    def _():
        st_ref[...] = s0_ref[...]

    lb = lb_ref[...]
    f = lb + (1.0 - lb) * jax.nn.sigmoid(f_ref[...])
    k_s[...] = 1.0 - f
    q = q_ref[...]
    q_s[...] = q * jax.nn.sigmoid(q)
    g = jnp.log(f)
    pos = lax.broadcasted_iota(jnp.int32, (tb, 1), 0) % HG_SUB
    sh = 1
    while sh < HG_SUB:
        if reverse:
            g = g + jnp.where(pos < HG_SUB - sh, pltpu.roll(g, tb - sh, 0), 0.0)
        else:
            g = g + jnp.where(pos >= sh, pltpu.roll(g, sh, 0), 0.0)
        sh *= 2
    g_s[...] = g

    row = lax.broadcasted_iota(jnp.int32, (HG_SUB, 1), 0)
    last = 0 if reverse else HG_SUB - 1
    nchunk = tb // HG_SUB

    def chunk(ci, st):
        c = nchunk - 1 - ci if reverse else ci
        rows = pl.ds(pl.multiple_of(c * HG_SUB, HG_SUB), HG_SUB)
        qc, kc, gc, vc = q_s[rows, :], k_s[rows, :], g_s[rows, :], v_ref[rows, :]
        gt = gc[last:last + 1, :]
        qs = qc * jnp.exp(gc)
        o = lax.dot_general(qs.astype(jnp.bfloat16), st.astype(jnp.bfloat16),
                            (((1,), (1,)), ((), ())), preferred_element_type=jnp.float32)
        for t in range(HG_SUB):
            e = jnp.exp(jnp.minimum(gc[t:t + 1, :] - gc, 0.0))
            a = jnp.sum(kc * e * qc[t:t + 1, :], axis=1, keepdims=True)
            a = jnp.where((row >= t) if reverse else (row <= t), a, 0.0)
            ot = jnp.sum(a * vc, axis=0, keepdims=True)
            o = o + jnp.where(row == t, ot, 0.0)
        o_s[rows, :] = o
        ks = kc * jnp.exp(gt - gc)
        upd = lax.dot_general(vc.astype(jnp.bfloat16), ks.astype(jnp.bfloat16),
                              (((0,), (0,)), ((), ())), preferred_element_type=jnp.float32)
        return st * jnp.exp(gt) + upd

    st_ref[...] = lax.fori_loop(0, nchunk, chunk, st_ref[...], unroll=2)

    if epilogue:
        o = o_s[...] + of_ref[...]
        y = o * lax.rsqrt(jnp.mean(o * o, axis=-1, keepdims=True) + EPS) * gn_ref[...]
        gate = gate_ref[...]
        o_ref[...] = y * (gate * jax.nn.sigmoid(gate))

    @pl.when(ti == pl.num_programs(2) - 1)
    def _():
        sfin_ref[...] = st_ref[...]


def hgrn2_scan(z, f_col, s0, lb, reverse, of=None, gn_g=None):
    bn, n, _ = z.shape
    nh, hd = HG_HEADS, HG_HEAD_DIM
    tb = min(HG_ROWS, n)
    nb = n // tb
    assert n % tb == 0 and tb % HG_SUB == 0
    epilogue = of is not None
    tix = (lambda t: nb - 1 - t) if reverse else (lambda t: t)

    def col(group):
        return pl.BlockSpec((None, tb, hd), lambda b, h, t: (b, tix(t), group * nh + h))

    o_spec = pl.BlockSpec((None, tb, hd), lambda b, h, t: (b, tix(t), h))
    s_spec = pl.BlockSpec((None, None, hd, hd), lambda b, h, t: (b, h, 0, 0))
    lb_spec = pl.BlockSpec((None, 1, hd), lambda b, h, t: (h, 0, 0))
    gn_spec = pl.BlockSpec((1, hd), lambda b, h, t: (0, 0))
    lb3 = lb.reshape(nh, 1, hd)
    scratch = [pltpu.VMEM((hd, hd), jnp.float32)] + [pltpu.VMEM((tb, hd), jnp.float32)] * 3
    if epilogue:
        in_specs = [col(3), col(f_col), col(6), col(7), o_spec, s_spec, lb_spec, gn_spec]
        args = (z, z, z, z, of, s0, lb3, gn_g.reshape(1, hd))
        scratch = scratch + [pltpu.VMEM((tb, hd), jnp.float32)]
    else:
        in_specs = [col(3), col(f_col), col(6), s_spec, lb_spec]
        args = (z, z, z, s0, lb3)
    return pl.pallas_call(
        partial(_hgrn2_kernel, reverse=reverse, epilogue=epilogue),
        grid=(bn, nh, nb),
        in_specs=in_specs,
        out_specs=[o_spec, s_spec],
        out_shape=[jax.ShapeDtypeStruct((bn, n, nh * hd), jnp.float32),
                   jax.ShapeDtypeStruct((bn, nh, hd, hd), jnp.float32)],
        scratch_shapes=scratch,
        compiler_params=pltpu.CompilerParams(
            dimension_semantics=("parallel", "parallel", "arbitrary")),
        name="hgrn2_" + ("rev" if reverse else "fwd") + ("_out" if epilogue else ""),
    )(*args)


NA_QROWS = 4
NA_UROWS = 12


def _na_kernel(q_ref, k_ref, v_ref, kc_ref, vc_ref, bias_ref, o_ref, *, rows):
    tq = NA_QROWS * GRID_W
    tk = NA_UROWS * GRID_W
    ngroups = rows // NA_QROWS
    scale = q_ref.shape[1] ** -0.5
    kc = kc_ref[...].astype(jnp.bfloat16)
    vc = vc_ref[...].astype(jnp.bfloat16)
    nt = (((1,), (1,)), ((), ()))

    def group(g, carry):
        q = q_ref[pl.ds(pl.multiple_of(g * tq, tq), tq), :].astype(jnp.bfloat16)
        u0 = jnp.clip(g * NA_QROWS - NA_KH // 2, 0, rows - NA_UROWS)
        krows = pl.ds(pl.multiple_of(u0 * GRID_W, GRID_W), tk)
        k_u = k_ref[krows, :].astype(jnp.bfloat16)
        v_u = v_ref[krows, :].astype(jnp.bfloat16)
        cls = jnp.where(g == 0, 0, jnp.where(g == ngroups - 1, 2, 1))
        s = lax.dot_general(q, k_u, nt, preferred_element_type=jnp.float32) * scale + bias_ref[cls]
        sc = lax.dot_general(q, kc, nt, preferred_element_type=jnp.float32) * scale
        m = jnp.maximum(jnp.max(s, axis=-1, keepdims=True), jnp.max(sc, axis=-1, keepdims=True))
        p = jnp.exp(s - m)
        pc = jnp.exp(sc - m)
        denom = jnp.sum(p, axis=-1, keepdims=True) + jnp.sum(pc, axis=-1, keepdims=True)
        o = (jnp.dot(p.astype(jnp.bfloat16), v_u, preferred_element_type=jnp.float32)
             + jnp.dot(pc.astype(jnp.bfloat16), vc, preferred_element_type=jnp.float32))
        o_ref[pl.ds(pl.multiple_of(g * tq, tq), tq), :] = o / denom
        return carry

    lax.fori_loop(0, ngroups, group, 0)


def na_bias_classes(table, rows):
    kh = min(NA_KH, rows)
    col = jnp.arange(GRID_W)
    c0 = jnp.clip(col - NA_KW // 2, 0, GRID_W - NA_KW)
    col_mask = (col[None, :] >= c0[:, None]) & (col[None, :] < c0[:, None] + NA_KW)
    dc_idx = jnp.clip(col[None, :] - col[:, None] + NA_KW - 1, 0, 2 * NA_KW - 2)
    table_c = jnp.take(table, dc_idx, axis=2)
    ngroups = rows // NA_QROWS
    out = []
    for g in (0, 1, ngroups - 1):
        rq = g * NA_QROWS + jnp.arange(NA_QROWS)
        r0 = jnp.clip(rq - kh // 2, 0, rows - kh)
        u0 = jnp.clip(g * NA_QROWS - NA_KH // 2, 0, rows - NA_UROWS)
        key_row = u0 + jnp.arange(NA_UROWS)
        valid = (key_row[None, :] >= r0[:, None]) & (key_row[None, :] < r0[:, None] + kh)
        dr = jnp.clip(key_row[None, :] - rq[:, None] + NA_KH - 1, 0, 2 * NA_KH - 2)
        b = table_c[:, dr]
        ok = valid[None, :, :, None, None] & col_mask[None, None, None, :, :]
        b = jnp.where(ok, b, NEG_INF).transpose(0, 1, 3, 2, 4)
        out.append(b.reshape(table.shape[0], NA_QROWS * GRID_W, NA_UROWS * GRID_W))
    return jnp.stack(out, axis=1)


def na_attention(z_l, z_c, table):
    bn, n, _ = z_l.shape
    nc = z_c.shape[1]
    nh = NA_HEADS
    dh = D_GROUP // nh
    rows = n // GRID_W
    assert rows % NA_QROWS == 0 and rows >= NA_UROWS and rows // NA_QROWS >= 3
    bias = na_bias_classes(table, rows)

    def col(nrow, group):
        return pl.BlockSpec((None, nrow, dh), lambda b, h: (b, 0, group * nh + h))

    return pl.pallas_call(
        partial(_na_kernel, rows=rows),
        grid=(bn, nh),
        in_specs=[col(n, 0), col(n, 1), col(n, 2), col(nc, 1), col(nc, 2),
                  pl.BlockSpec((None, 3, NA_QROWS * GRID_W, NA_UROWS * GRID_W), lambda b, h: (h, 0, 0, 0))],
        out_specs=pl.BlockSpec((None, n, dh), lambda b, h: (b, 0, h)),
        out_shape=jax.ShapeDtypeStruct((bn, n, nh * dh), jnp.float32),
        compiler_params=pltpu.CompilerParams(
            dimension_semantics=("parallel", "parallel"),
            vmem_limit_bytes=V7X_VMEM_LIMIT_BYTES),
        name="na_attention",
    )(z_l, z_l, z_l, z_c, z_c, bias)


def rmsnorm(x, g):
    y = x * lax.rsqrt(jnp.mean(x * x, axis=-1, keepdims=True) + EPS)
    return y * g


def layernorm(x, g, b):
    mu = jnp.mean(x, axis=-1, keepdims=True)
    var = jnp.mean(jnp.square(x - mu), axis=-1, keepdims=True)
    return (x - mu) * lax.rsqrt(var + EPS) * g + b


def modulate(h, shift, scale):
    return h * (1.0 + scale) + shift


def split_heads(t, n_heads):
    b, n, _ = t.shape
    return t.reshape(b, n, n_heads, -1).transpose(0, 2, 1, 3)


def merge_heads(t):
    b, h, n, d = t.shape
    return t.transpose(0, 2, 1, 3).reshape(b, n, h * d)


def depthwise_conv(x, w, b, pad_left, pad_right):
    y = lax.conv_general_dilated(
        x, w[:, None, :].astype(x.dtype), (1,), [(pad_left, pad_right)],
        dimension_numbers=("NWC", "WIO", "NWC"), feature_group_count=x.shape[-1])
    return y + b


def rglru_coeffs(x, w_a, b_a, w_i, b_i, lam):
    bn, n, _ = x.shape
    xb = x.reshape(bn, n, LRU_BLOCKS, LRU_BLOCK_DIM)
    r = jax.nn.sigmoid(jnp.einsum("bnhi,hij->bnhj", xb, w_a).reshape(bn, n, D_GROUP) + b_a)
    i = jax.nn.sigmoid(jnp.einsum("bnhi,hij->bnhj", xb, w_i).reshape(bn, n, D_GROUP) + b_i)
    log_a = (-LRU_C * r) * jax.nn.softplus(-lam)
    a = jnp.exp(log_a)
    b = jnp.sqrt(-jnp.expm1(2.0 * log_a)) * (i * x)
    return a, b


def rglru_direction(xl, xc, w_a, b_a, w_i, b_i, lam, reverse):
    ac, bc = rglru_coeffs(xc, w_a, b_a, w_i, b_i, lam)
    hc = lru_scan(ac, bc, jnp.zeros_like(bc[:, 0]), reverse)
    al, bl = rglru_coeffs(xl, w_a, b_a, w_i, b_i, lam)
    hl = lru_scan(al, bl, hc[:, 0] if reverse else hc[:, -1], reverse)
    return hl, hc


def ab_mixer(h, hc, w_in, w_out, dw_w, dw_b, ln_g, ln_b, cv_w, cv_b, wa, ba, wi, bi, lam, need_ctx):
    za, zg, zy, zx = jnp.split(h @ w_in, 4, axis=-1)
    ca, cg, cy, cx = jnp.split(mm3(hc, w_in), 4, axis=-1)

    def conformer(a_val, a_gate):
        u = a_val * jax.nn.sigmoid(a_gate)
        u = depthwise_conv(u, dw_w, dw_b, CONF_KERNEL // 2, CONF_KERNEL // 2)
        return jax.nn.silu(layernorm(u, ln_g, ln_b))

    xl = depthwise_conv(zx, cv_w, cv_b, 2, 1)
    xcc = depthwise_conv(cx, cv_w, cv_b, 2, 1)
    f_l, f_c = rglru_direction(xl, xcc, wa[0], ba[0], wi[0], bi[0], lam[0], False)
    b_l, b_c = rglru_direction(xl, xcc, wa[1], ba[1], wi[1], bi[1], lam[1], True)
    out_l = mm3(jnp.concatenate([conformer(za, zg), jax.nn.gelu(zy) * (f_l + b_l)], axis=-1), w_out)
    if not need_ctx:
        return out_l, None
    out_c = mm3(jnp.concatenate([conformer(ca, cg), jax.nn.gelu(cy) * (f_c + b_c)], axis=-1), w_out)
    return out_l, out_c


def dense_attention(q, k, v):
    s = jnp.einsum("bhqd,bhkd->bhqk", q, k) * (q.shape[-1] ** -0.5)
    return jnp.einsum("bhqk,bhkd->bhqd", jax.nn.softmax(s, axis=-1), v)


def hgrn_lower_bound(lb_logits, layer):
    p = jax.nn.softmax(lb_logits, axis=0)
    return jnp.cumsum(p, axis=0)[layer] - p[0]


def cd_mixer(h, hc, w_in, w_out, table, lb_f, lb_b, gn_g, need_ctx):
    z_l = mm3(h, w_in)
    z_c = mm3(hc, w_in)
    o_na = na_attention(z_l, z_c, table)
    s0 = jnp.zeros((h.shape[0], HG_HEADS, HG_HEAD_DIM, HG_HEAD_DIM), jnp.float32)
    of_c, s_f = hgrn2_scan(z_c, 4, s0, lb_f, False)
    of_l, _ = hgrn2_scan(z_l, 4, s_f, lb_f, False)
    if need_ctx:
        ho_c, s_b = hgrn2_scan(z_c, 5, s0, lb_b, True, of=of_c, gn_g=gn_g)
    else:
        _, s_b = hgrn2_scan(z_c, 5, s0, lb_b, True)
    ho_l, _ = hgrn2_scan(z_l, 5, s_b, lb_b, True, of=of_l, gn_g=gn_g)
    out_l = mm3(jnp.concatenate([o_na, ho_l], axis=-1), w_out)
    if not need_ctx:
        return out_l, None
    qnc, knc, vnc = (split_heads(t, NA_HEADS) for t in jnp.split(z_c, 8, axis=-1)[:3])
    o_na_c = merge_heads(dense_attention(qnc, knc, vnc))
    out_c = mm3(jnp.concatenate([o_na_c, ho_c], axis=-1), w_out)
    return out_l, out_c


def ec_moe(h, router, w1, w3, w2, layer):
    bn, n, d = h.shape
    cap = EC_FACTOR * n // N_EXPERTS
    logits = jnp.einsum("bnd,de->bne", h, router, precision=lax.Precision.HIGHEST)
    aff = jax.nn.softmax(logits, axis=-1)
    gate, idx = lax.top_k(jnp.swapaxes(aff, 1, 2), cap)
    bidx = jnp.arange(bn)[:, None, None]
    xs = h.astype(jnp.bfloat16)[bidx, idx]
    xs_e = xs.transpose(1, 0, 2, 3).reshape(N_EXPERTS, bn * cap, d)
    gate_e = gate.transpose(1, 0, 2).reshape(N_EXPERTS, bn * cap, 1)
    y = expert_ffn(xs_e, w1, w3, w2, gate_e, layer)
    y = y.reshape(N_EXPERTS, bn, cap, d).transpose(1, 0, 2, 3)
    return jnp.zeros_like(h).at[bidx, idx].add(y)


def kernel(x, c, ctx, c_ctx, ada_w, ada_b, norm_g, final_g, ab_w_in, ab_w_out, conf_dw_w, conf_dw_b, conf_ln_g, conf_ln_b, lru_conv_w, lru_conv_b, lru_wa, lru_ba, lru_wi, lru_bi, lru_lam, cd_w_in, cd_w_out, na_bias, hgrn_lb, hgrn_norm_g, moe_router, moe_w1, moe_w3, moe_w2):
    xc = ctx
    s_lat = jax.nn.silu(c)
    s_ctx = jax.nn.silu(c_ctx)
    for l in range(DEPTH):
        last = l == DEPTH - 1
        j = l // 2
        mod_l = jnp.split((s_lat @ ada_w[l] + ada_b[l])[:, None, :], 6, axis=-1)
        mod_c = jnp.split(s_ctx @ ada_w[l] + ada_b[l], 6, axis=-1)
        h = modulate(rmsnorm(x, norm_g[l, 0]), mod_l[0], mod_l[1])
        hc = modulate(rmsnorm(xc, norm_g[l, 0]), mod_c[0], mod_c[1])
        if l % 2 == 0:
            o, oc = ab_mixer(h, hc, ab_w_in[j], ab_w_out[j], conf_dw_w[j], conf_dw_b[j],
                             conf_ln_g[j], conf_ln_b[j], lru_conv_w[j], lru_conv_b[j],
                             lru_wa[j], lru_ba[j], lru_wi[j], lru_bi[j], lru_lam[j], not last)
        else:
            lb_f = hgrn_lower_bound(hgrn_lb[0], l)
            lb_b = hgrn_lower_bound(hgrn_lb[1], l)
            o, oc = cd_mixer(h, hc, cd_w_in[j], cd_w_out[j], na_bias[j], lb_f, lb_b,
                             hgrn_norm_g[j], not last)
        x = x + mod_l[2] * o
        h = modulate(rmsnorm(x, norm_g[l, 1]), mod_l[3], mod_l[4])
        x = x + mod_l[5] * ec_moe(h, moe_router[l], moe_w1, moe_w3, moe_w2, l)
        if not last:
            xc = xc + mod_c[2] * oc
            hc = modulate(rmsnorm(xc, norm_g[l, 1]), mod_c[3], mod_c[4])
            xc = xc + mod_c[5] * ec_moe(hc, moe_router[l], moe_w1, moe_w3, moe_w2, l)
    return rmsnorm(x, final_g)
```

```python
from functools import partial

import jax
import jax.numpy as jnp
from jax import lax
from jax.experimental import pallas as pl
from jax.experimental.pallas import tpu as pltpu

D_MODEL = 2048
DEPTH = 2
GRID_W = 64
D_GROUP = D_MODEL // 2
CONF_KERNEL = 31
LRU_BLOCKS = 16
LRU_BLOCK_DIM = D_GROUP // LRU_BLOCKS
LRU_C = 8.0
NA_HEADS = 8
NA_KH = 8
NA_KW = 16
HG_HEADS = 8
HG_HEAD_DIM = D_GROUP // HG_HEADS
HG_CHUNK = 64
N_EXPERTS = 16
EC_FACTOR = 2
EPS = 1e-6
NEG_INF = -1e30

V7X_VMEM_LIMIT_BYTES = 60000 * 1024


def _matmul_kernel(x_ref, w_ref, o_ref, xb_ref):
    @pl.when(pl.program_id(1) == 0)
    def _():
        xb_ref[...] = x_ref[...].astype(jnp.bfloat16)

    o_ref[...] = jnp.dot(xb_ref[...], w_ref[...].astype(jnp.bfloat16),
                         preferred_element_type=jnp.float32)


def matmul(x, w, tm=1024, tn=512):
    m, k = x.shape
    n = w.shape[1]
    tm = min(tm, m)
    tn = min(tn, n)
    assert m % tm == 0 and n % tn == 0
    return pl.pallas_call(
        _matmul_kernel,
        grid=(m // tm, n // tn),
        in_specs=[pl.BlockSpec((tm, k), lambda i, j: (i, 0)),
                  pl.BlockSpec((k, tn), lambda i, j: (0, j))],
        out_specs=pl.BlockSpec((tm, tn), lambda i, j: (i, j)),
        out_shape=jax.ShapeDtypeStruct((m, n), jnp.float32),
        scratch_shapes=[pltpu.VMEM((tm, k), jnp.bfloat16)],
        compiler_params=pltpu.CompilerParams(
            dimension_semantics=("parallel", "arbitrary"),
            vmem_limit_bytes=V7X_VMEM_LIMIT_BYTES),
        name="dense_matmul",
    )(x, w)


def mm3(x, w):
    b, n, k = x.shape
    return matmul(x.reshape(b * n, k), w).reshape(b, n, w.shape[1])


def _ffn_kernel(xs_ref, w1_ref, w3_ref, w2_ref, g_ref, o_ref):
    f = pl.program_id(2)
    xs = xs_ref[...]
    a = jnp.dot(xs, w1_ref[...].astype(jnp.bfloat16), preferred_element_type=jnp.float32)
    b = jnp.dot(xs, w3_ref[...].astype(jnp.bfloat16), preferred_element_type=jnp.float32)
    hid = (a * jax.nn.sigmoid(a)) * b
    y = jnp.dot(hid.astype(jnp.bfloat16), w2_ref[...].astype(jnp.bfloat16),
                preferred_element_type=jnp.float32)

    @pl.when(f == 0)
    def _():
        o_ref[...] = y

    @pl.when(f != 0)
    def _():
        o_ref[...] += y

    @pl.when(f == pl.num_programs(2) - 1)
    def _():
        o_ref[...] = o_ref[...] * g_ref[...]


def expert_ffn(xs, w1, w3, w2, gate, layer, tf=512):
    e, m, d = xs.shape
    f = w1.shape[3]
    tm = min(1024, m)
    assert m % tm == 0 and f % tf == 0
    return pl.pallas_call(
        _ffn_kernel,
        grid=(e, m // tm, f // tf),
        in_specs=[pl.BlockSpec((None, tm, d), lambda ei, mi, fi: (ei, mi, 0)),
                  pl.BlockSpec((None, None, d, tf), lambda ei, mi, fi: (layer, ei, 0, fi)),
                  pl.BlockSpec((None, None, d, tf), lambda ei, mi, fi: (layer, ei, 0, fi)),
                  pl.BlockSpec((None, None, tf, d), lambda ei, mi, fi: (layer, ei, fi, 0)),
                  pl.BlockSpec((None, tm, 1), lambda ei, mi, fi: (ei, mi, 0))],
        out_specs=pl.BlockSpec((None, tm, d), lambda ei, mi, fi: (ei, mi, 0)),
        out_shape=jax.ShapeDtypeStruct((e, m, d), jnp.float32),
        compiler_params=pltpu.CompilerParams(
            dimension_semantics=("parallel", "parallel", "arbitrary"),
            vmem_limit_bytes=V7X_VMEM_LIMIT_BYTES),
        name="expert_ffn",
    )(xs, w1, w3, w2, gate)


LRU_SCAN_ROWS = 256


def _lru_scan_kernel(a_ref, b_ref, h0_ref, o_ref, carry_ref, *, reverse):
    tn = a_ref.shape[1]

    @pl.when(pl.program_id(0) == 0)
    def _():
        carry_ref[...] = h0_ref[...]

    def body(t, h):
        r = tn - 1 - t if reverse else t
        h = a_ref[:, pl.ds(r, 1), :] * h + b_ref[:, pl.ds(r, 1), :]
        o_ref[:, pl.ds(r, 1), :] = h
        return h

    carry_ref[...] = lax.fori_loop(0, tn, body, carry_ref[...], unroll=8)


def lru_scan(a, b, h0, reverse):
    bn, n, c = a.shape
    tn = min(LRU_SCAN_ROWS, n)
    nb = n // tn
    assert n % tn == 0
    idx = (lambda i: (0, nb - 1 - i, 0)) if reverse else (lambda i: (0, i, 0))
    return pl.pallas_call(
        partial(_lru_scan_kernel, reverse=reverse),
        grid=(nb,),
        in_specs=[pl.BlockSpec((bn, tn, c), idx),
                  pl.BlockSpec((bn, tn, c), idx),
                  pl.BlockSpec((bn, 1, c), lambda i: (0, 0, 0))],
        out_specs=pl.BlockSpec((bn, tn, c), idx),
        out_shape=jax.ShapeDtypeStruct((bn, n, c), jnp.float32),
        scratch_shapes=[pltpu.VMEM((bn, 1, c), jnp.float32)],
        compiler_params=pltpu.CompilerParams(
            dimension_semantics=("arbitrary",),
            vmem_limit_bytes=V7X_VMEM_LIMIT_BYTES),
        name="lru_scan_rev" if reverse else "lru_scan_fwd",
    )(a, b, h0[:, None, :])


HG_SUB = 32
HG_ROWS = 512


def _hgrn2_kernel(*refs, reverse, epilogue):
    if epilogue:
        (q_ref, f_ref, v_ref, gate_ref, of_ref, s0_ref, lb_ref, gn_ref,
         o_ref, sfin_ref, st_ref, q_s, k_s, g_s, o_s) = refs
    else:
        (q_ref, f_ref, v_ref, s0_ref, lb_ref,
         o_ref, sfin_ref, st_ref, q_s, k_s, g_s) = refs
        o_s = o_ref
    tb = q_ref.shape[0]
    ti = pl.program_id(2)

    @pl.when(ti == 0)
    def _():
        st_ref[...] = s0_ref[...]

    lb = lb_ref[...]
    f = lb + (1.0 - lb) * jax.nn.sigmoid(f_ref[...])
    k_s[...] = 1.0 - f
    q = q_ref[...]
    q_s[...] = q * jax.nn.sigmoid(q)
    g = jnp.log(f)
    pos = lax.broadcasted_iota(jnp.int32, (tb, 1), 0) % HG_SUB
    sh = 1
    while sh < HG_SUB:
        if reverse:
            g = g + jnp.where(pos < HG_SUB - sh, pltpu.roll(g, tb - sh, 0), 0.0)
        else:
            g = g + jnp.where(pos >= sh, pltpu.roll(g, sh, 0), 0.0)
        sh *= 2
    g_s[...] = g

    row = lax.broadcasted_iota(jnp.int32, (HG_SUB, 1), 0)
    last = 0 if reverse else HG_SUB - 1
    nchunk = tb // HG_SUB

    def chunk(ci, st):
        c = nchunk - 1 - ci if reverse else ci
        rows = pl.ds(pl.multiple_of(c * HG_SUB, HG_SUB), HG_SUB)
        qc, kc, gc, vc = q_s[rows, :], k_s[rows, :], g_s[rows, :], v_ref[rows, :]
        gt = gc[last:last + 1, :]
        qs = qc * jnp.exp(gc)
        o = lax.dot_general(qs.astype(jnp.bfloat16), st.astype(jnp.bfloat16),
                            (((1,), (1,)), ((), ())), preferred_element_type=jnp.float32)
        for t in range(HG_SUB):
            e = jnp.exp(jnp.minimum(gc[t:t + 1, :] - gc, 0.0))
            a = jnp.sum(kc * e * qc[t:t + 1, :], axis=1, keepdims=True)
            a = jnp.where((row >= t) if reverse else (row <= t), a, 0.0)
            ot = jnp.sum(a * vc, axis=0, keepdims=True)
            o = o + jnp.where(row == t, ot, 0.0)
        o_s[rows, :] = o
        ks = kc * jnp.exp(gt - gc)
        upd = lax.dot_general(vc.astype(jnp.bfloat16), ks.astype(jnp.bfloat16),
                              (((0,), (0,)), ((), ())), preferred_element_type=jnp.float32)
        return st * jnp.exp(gt) + upd

    st_ref[...] = lax.fori_loop(0, nchunk, chunk, st_ref[...], unroll=2)

    if epilogue:
        o = o_s[...] + of_ref[...]
        y = o * lax.rsqrt(jnp.mean(o * o, axis=-1, keepdims=True) + EPS) * gn_ref[...]
        gate = gate_ref[...]
        o_ref[...] = y * (gate * jax.nn.sigmoid(gate))

    @pl.when(ti == pl.num_programs(2) - 1)
    def _():
        sfin_ref[...] = st_ref[...]


def hgrn2_scan(z, f_col, s0, lb, reverse, of=None, gn_g=None):
    bn, n, _ = z.shape
    nh, hd = HG_HEADS, HG_HEAD_DIM
    tb = min(HG_ROWS, n)
    nb = n // tb
    assert n % tb == 0 and tb % HG_SUB == 0
    epilogue = of is not None
    tix = (lambda t: nb - 1 - t) if reverse else (lambda t: t)

    def col(group):
        return pl.BlockSpec((None, tb, hd), lambda b, h, t: (b, tix(t), group * nh + h))

    o_spec = pl.BlockSpec((None, tb, hd), lambda b, h, t: (b, tix(t), h))
    s_spec = pl.BlockSpec((None, None, hd, hd), lambda b, h, t: (b, h, 0, 0))
    lb_spec = pl.BlockSpec((None, 1, hd), lambda b, h, t: (h, 0, 0))
    gn_spec = pl.BlockSpec((1, hd), lambda b, h, t: (0, 0))
    lb3 = lb.reshape(nh, 1, hd)
    scratch = [pltpu.VMEM((hd, hd), jnp.float32)] + [pltpu.VMEM((tb, hd), jnp.float32)] * 3
    if epilogue:
        in_specs = [col(3), col(f_col), col(6), col(7), o_spec, s_spec, lb_spec, gn_spec]
        args = (z, z, z, z, of, s0, lb3, gn_g.reshape(1, hd))
        scratch = scratch + [pltpu.VMEM((tb, hd), jnp.float32)]
    else:
        in_specs = [col(3), col(f_col), col(6), s_spec, lb_spec]
        args = (z, z, z, s0, lb3)
    return pl.pallas_call(
        partial(_hgrn2_kernel, reverse=reverse, epilogue=epilogue),
        grid=(bn, nh, nb),
        in_specs=in_specs,
        out_specs=[o_spec, s_spec],
        out_shape=[jax.ShapeDtypeStruct((bn, n, nh * hd), jnp.float32),
                   jax.ShapeDtypeStruct((bn, nh, hd, hd), jnp.float32)],
        scratch_shapes=scratch,
        compiler_params=pltpu.CompilerParams(
            dimension_semantics=("parallel", "parallel", "arbitrary")),
        name="hgrn2_" + ("rev" if reverse else "fwd") + ("_out" if epilogue else ""),
    )(*args)


NA_QROWS = 4
NA_UROWS = 12


def _na_kernel(q_ref, k_ref, v_ref, kc_ref, vc_ref, bias_ref, o_ref, *, rows):
    tq = NA_QROWS * GRID_W
    tk = NA_UROWS * GRID_W
    ngroups = rows // NA_QROWS
    scale = q_ref.shape[1] ** -0.5
    kc = kc_ref[...].astype(jnp.bfloat16)
    vc = vc_ref[...].astype(jnp.bfloat16)
    nt = (((1,), (1,)), ((), ()))

    def group(g, carry):
        q = q_ref[pl.ds(pl.multiple_of(g * tq, tq), tq), :].astype(jnp.bfloat16)
        u0 = jnp.clip(g * NA_QROWS - NA_KH // 2, 0, rows - NA_UROWS)
        krows = pl.ds(pl.multiple_of(u0 * GRID_W, GRID_W), tk)
        k_u = k_ref[krows, :].astype(jnp.bfloat16)
        v_u = v_ref[krows, :].astype(jnp.bfloat16)
        cls = jnp.where(g == 0, 0, jnp.where(g == ngroups - 1, 2, 1))
        s = lax.dot_general(q, k_u, nt, preferred_element_type=jnp.float32) * scale + bias_ref[cls]
        sc = lax.dot_general(q, kc, nt, preferred_element_type=jnp.float32) * scale
        m = jnp.maximum(jnp.max(s, axis=-1, keepdims=True), jnp.max(sc, axis=-1, keepdims=True))
        p = jnp.exp(s - m)
        pc = jnp.exp(sc - m)
        denom = jnp.sum(p, axis=-1, keepdims=True) + jnp.sum(pc, axis=-1, keepdims=True)
        o = (jnp.dot(p.astype(jnp.bfloat16), v_u, preferred_element_type=jnp.float32)
             + jnp.dot(pc.astype(jnp.bfloat16), vc, preferred_element_type=jnp.float32))
        o_ref[pl.ds(pl.multiple_of(g * tq, tq), tq), :] = o / denom
        return carry

    lax.fori_loop(0, ngroups, group, 0)


def na_bias_classes(table, rows):
    kh = min(NA_KH, rows)
    col = jnp.arange(GRID_W)
    c0 = jnp.clip(col - NA_KW // 2, 0, GRID_W - NA_KW)
    col_mask = (col[None, :] >= c0[:, None]) & (col[None, :] < c0[:, None] + NA_KW)
    dc_idx = jnp.clip(col[None, :] - col[:, None] + NA_KW - 1, 0, 2 * NA_KW - 2)
    table_c = jnp.take(table, dc_idx, axis=2)
    ngroups = rows // NA_QROWS
    out = []
    for g in (0, 1, ngroups - 1):
        rq = g * NA_QROWS + jnp.arange(NA_QROWS)
        r0 = jnp.clip(rq - kh // 2, 0, rows - kh)
        u0 = jnp.clip(g * NA_QROWS - NA_KH // 2, 0, rows - NA_UROWS)
        key_row = u0 + jnp.arange(NA_UROWS)
        valid = (key_row[None, :] >= r0[:, None]) & (key_row[None, :] < r0[:, None] + kh)
        dr = jnp.clip(key_row[None, :] - rq[:, None] + NA_KH - 1, 0, 2 * NA_KH - 2)
        b = table_c[:, dr]
        ok = valid[None, :, :, None, None] & col_mask[None, None, None, :, :]
        b = jnp.where(ok, b, NEG_INF).transpose(0, 1, 3, 2, 4)
        out.append(b.reshape(table.shape[0], NA_QROWS * GRID_W, NA_UROWS * GRID_W))
    return jnp.stack(out, axis=1)


def na_attention(z_l, z_c, table):
    bn, n, _ = z_l.shape
    nc = z_c.shape[1]
    nh = NA_HEADS
    dh = D_GROUP // nh
    rows = n // GRID_W
    assert rows % NA_QROWS == 0 and rows >= NA_UROWS and rows // NA_QROWS >= 3
    bias = na_bias_classes(table, rows)

    def col(nrow, group):
        return pl.BlockSpec((None, nrow, dh), lambda b, h: (b, 0, group * nh + h))

    return pl.pallas_call(
        partial(_na_kernel, rows=rows),
        grid=(bn, nh),
        in_specs=[col(n, 0), col(n, 1), col(n, 2), col(nc, 1), col(nc, 2),
                  pl.BlockSpec((None, 3, NA_QROWS * GRID_W, NA_UROWS * GRID_W), lambda b, h: (h, 0, 0, 0))],
        out_specs=pl.BlockSpec((None, n, dh), lambda b, h: (b, 0, h)),
        out_shape=jax.ShapeDtypeStruct((bn, n, nh * dh), jnp.float32),
        compiler_params=pltpu.CompilerParams(
            dimension_semantics=("parallel", "parallel"),
            vmem_limit_bytes=V7X_VMEM_LIMIT_BYTES),
        name="na_attention",
    )(z_l, z_l, z_l, z_c, z_c, bias)


def rmsnorm(x, g):
    y = x * lax.rsqrt(jnp.mean(x * x, axis=-1, keepdims=True) + EPS)
    return y * g


def layernorm(x, g, b):
    mu = jnp.mean(x, axis=-1, keepdims=True)
    var = jnp.mean(jnp.square(x - mu), axis=-1, keepdims=True)
    return (x - mu) * lax.rsqrt(var + EPS) * g + b


def modulate(h, shift, scale):
    return h * (1.0 + scale) + shift


def split_heads(t, n_heads):
    b, n, _ = t.shape
    return t.reshape(b, n, n_heads, -1).transpose(0, 2, 1, 3)


def merge_heads(t):
    b, h, n, d = t.shape
    return t.transpose(0, 2, 1, 3).reshape(b, n, h * d)


def depthwise_conv(x, w, b, pad_left, pad_right):
    y = lax.conv_general_dilated(
        x, w[:, None, :].astype(x.dtype), (1,), [(pad_left, pad_right)],
        dimension_numbers=("NWC", "WIO", "NWC"), feature_group_count=x.shape[-1])
    return y + b


def rglru_coeffs(x, w_a, b_a, w_i, b_i, lam):
    bn, n, _ = x.shape
    xb = x.reshape(bn, n, LRU_BLOCKS, LRU_BLOCK_DIM)
    r = jax.nn.sigmoid(jnp.einsum("bnhi,hij->bnhj", xb, w_a).reshape(bn, n, D_GROUP) + b_a)
    i = jax.nn.sigmoid(jnp.einsum("bnhi,hij->bnhj", xb, w_i).reshape(bn, n, D_GROUP) + b_i)
    log_a = (-LRU_C * r) * jax.nn.softplus(-lam)
    a = jnp.exp(log_a)
    b = jnp.sqrt(-jnp.expm1(2.0 * log_a)) * (i * x)
    return a, b


def rglru_direction(xl, xc, w_a, b_a, w_i, b_i, lam, reverse):
    ac, bc = rglru_coeffs(xc, w_a, b_a, w_i, b_i, lam)
    hc = lru_scan(ac, bc, jnp.zeros_like(bc[:, 0]), reverse)
    al, bl = rglru_coeffs(xl, w_a, b_a, w_i, b_i, lam)
    hl = lru_scan(al, bl, hc[:, 0] if reverse else hc[:, -1], reverse)
    return hl, hc


def ab_mixer(h, hc, w_in, w_out, dw_w, dw_b, ln_g, ln_b, cv_w, cv_b, wa, ba, wi, bi, lam, need_ctx):
    za, zg, zy, zx = jnp.split(h @ w_in, 4, axis=-1)
    ca, cg, cy, cx = jnp.split(mm3(hc, w_in), 4, axis=-1)

    def conformer(a_val, a_gate):
        u = a_val * jax.nn.sigmoid(a_gate)
        u = depthwise_conv(u, dw_w, dw_b, CONF_KERNEL // 2, CONF_KERNEL // 2)
        return jax.nn.silu(layernorm(u, ln_g, ln_b))

    xl = depthwise_conv(zx, cv_w, cv_b, 2, 1)
    xcc = depthwise_conv(cx, cv_w, cv_b, 2, 1)
    f_l, f_c = rglru_direction(xl, xcc, wa[0], ba[0], wi[0], bi[0], lam[0], False)
    b_l, b_c = rglru_direction(xl, xcc, wa[1], ba[1], wi[1], bi[1], lam[1], True)
    out_l = mm3(jnp.concatenate([conformer(za, zg), jax.nn.gelu(zy) * (f_l + b_l)], axis=-1), w_out)
    if not need_ctx:
        return out_l, None
    out_c = mm3(jnp.concatenate([conformer(ca, cg), jax.nn.gelu(cy) * (f_c + b_c)], axis=-1), w_out)
    return out_l, out_c


def dense_attention(q, k, v):
    s = jnp.einsum("bhqd,bhkd->bhqk", q, k) * (q.shape[-1] ** -0.5)
    return jnp.einsum("bhqk,bhkd->bhqd", jax.nn.softmax(s, axis=-1), v)


def hgrn_lower_bound(lb_logits, layer):
    p = jax.nn.softmax(lb_logits, axis=0)
    return jnp.cumsum(p, axis=0)[layer] - p[0]


def cd_mixer(h, hc, w_in, w_out, table, lb_f, lb_b, gn_g, need_ctx):
    z_l = mm3(h, w_in)
    z_c = mm3(hc, w_in)
    o_na = na_attention(z_l, z_c, table)
    s0 = jnp.zeros((h.shape[0], HG_HEADS, HG_HEAD_DIM, HG_HEAD_DIM), jnp.float32)
    of_c, s_f = hgrn2_scan(z_c, 4, s0, lb_f, False)
    of_l, _ = hgrn2_scan(z_l, 4, s_f, lb_f, False)
    if need_ctx:
        ho_c, s_b = hgrn2_scan(z_c, 5, s0, lb_b, True, of=of_c, gn_g=gn_g)
    else:
        _, s_b = hgrn2_scan(z_c, 5, s0, lb_b, True)
    ho_l, _ = hgrn2_scan(z_l, 5, s_b, lb_b, True, of=of_l, gn_g=gn_g)
    out_l = mm3(jnp.concatenate([o_na, ho_l], axis=-1), w_out)
    if not need_ctx:
        return out_l, None
    qnc, knc, vnc = (split_heads(t, NA_HEADS) for t in jnp.split(z_c, 8, axis=-1)[:3])
    o_na_c = merge_heads(dense_attention(qnc, knc, vnc))
    out_c = mm3(jnp.concatenate([o_na_c, ho_c], axis=-1), w_out)
    return out_l, out_c


def ec_moe(h, router, w1, w3, w2, layer):
    bn, n, d = h.shape
    cap = EC_FACTOR * n // N_EXPERTS
    logits = jnp.einsum("bnd,de->bne", h, router, precision=lax.Precision.HIGHEST)
    aff = jax.nn.softmax(logits, axis=-1)
    gate, idx = lax.top_k(jnp.swapaxes(aff, 1, 2), cap)
    bidx = jnp.arange(bn)[:, None, None]
    xs = h.astype(jnp.bfloat16)[bidx, idx]
    xs_e = xs.transpose(1, 0, 2, 3).reshape(N_EXPERTS, bn * cap, d)
    gate_e = gate.transpose(1, 0, 2).reshape(N_EXPERTS, bn * cap, 1)
    y = expert_ffn(xs_e, w1, w3, w2, gate_e, layer)
    y = y.reshape(N_EXPERTS, bn, cap, d).transpose(1, 0, 2, 3)
    return jnp.zeros_like(h).at[bidx, idx].add(y)


def kernel(x, c, ctx, c_ctx, ada_w, ada_b, norm_g, final_g, ab_w_in, ab_w_out, conf_dw_w, conf_dw_b, conf_ln_g, conf_ln_b, lru_conv_w, lru_conv_b, lru_wa, lru_ba, lru_wi, lru_bi, lru_lam, cd_w_in, cd_w_out, na_bias, hgrn_lb, hgrn_norm_g, moe_router, moe_w1, moe_w3, moe_w2):
    xc = ctx
    s_lat = jax.nn.silu(c)
    s_ctx = jax.nn.silu(c_ctx)
    for l in range(DEPTH):
        last = l == DEPTH - 1
        j = l // 2
        mod_l = jnp.split((s_lat @ ada_w[l] + ada_b[l])[:, None, :], 6, axis=-1)
        mod_c = jnp.split(s_ctx @ ada_w[l] + ada_b[l], 6, axis=-1)
        h = modulate(rmsnorm(x, norm_g[l, 0]), mod_l[0], mod_l[1])
        hc = modulate(rmsnorm(xc, norm_g[l, 0]), mod_c[0], mod_c[1])
        if l % 2 == 0:
            o, oc = ab_mixer(h, hc, ab_w_in[j], ab_w_out[j], conf_dw_w[j], conf_dw_b[j],
                             conf_ln_g[j], conf_ln_b[j], lru_conv_w[j], lru_conv_b[j],
                             lru_wa[j], lru_ba[j], lru_wi[j], lru_bi[j], lru_lam[j], not last)
        else:
            lb_f = hgrn_lower_bound(hgrn_lb[0], l)
            lb_b = hgrn_lower_bound(hgrn_lb[1], l)
            o, oc = cd_mixer(h, hc, cd_w_in[j], cd_w_out[j], na_bias[j], lb_f, lb_b,
                             hgrn_norm_g[j], not last)
        x = x + mod_l[2] * o
        h = modulate(rmsnorm(x, norm_g[l, 1]), mod_l[3], mod_l[4])
        x = x + mod_l[5] * ec_moe(h, moe_router[l], moe_w1, moe_w3, moe_w2, l)
        if not last:
            xc = xc + mod_c[2] * oc
            hc = modulate(rmsnorm(xc, norm_g[l, 1]), mod_c[3], mod_c[4])
            xc = xc + mod_c[5] * ec_moe(hc, moe_router[l], moe_w1, moe_w3, moe_w2, l)
    return rmsnorm(x, final_g)
```
